```python
import jax, jax.numpy as jnp
from jax import lax
import numpy as np

D_MODEL = 1024
BATCH = 8
SEQ = 2048
DEPTH = 4

GRID_W = 64
CTX_LEN = 256
N_MIXERS = 2
GQA_HEADS = 16
GQA_KV_HEADS = 4
GQA_GROUP = GQA_HEADS // GQA_KV_HEADS
GQA_HEAD_DIM = D_MODEL // GQA_HEADS
MLA_HEADS = 16
MLA_Q_RANK = (3 * D_MODEL) // 8
MLA_KV_RANK = D_MODEL // 4
MLA_NOPE_DIM = 64
MLA_ROPE_DIM = 32
MLA_V_DIM = 64
FFN_HIDDEN = 4 * D_MODEL
ROPE_THETA = 10000.0
Q_BLOCK = 128
NORM_EPS = 1e-6
DEEPNORM_ALPHA = (2.0 * DEPTH) ** 0.25
DEEPNORM_BETA = (8.0 * DEPTH) ** -0.25

kernel_name = "hybrid_gqa_mla_deepnorm_dit"


def layer_norm(x, g, b):
    xf = x.astype(jnp.float32)
    mu = jnp.mean(xf, axis=-1, keepdims=True)
    var = jnp.mean(jnp.square(xf - mu), axis=-1, keepdims=True)
    return ((xf - mu) * lax.rsqrt(var + NORM_EPS) * g.astype(jnp.float32) + b.astype(jnp.float32)).astype(x.dtype)


def rms_norm(x, g):
    xf = x.astype(jnp.float32)
    ms = jnp.mean(jnp.square(xf), axis=-1, keepdims=True)
    return (xf * lax.rsqrt(ms + NORM_EPS) * g.astype(jnp.float32)).astype(x.dtype)


def grid_rope_tables(n_tok, rot_dim):
    rows = n_tok // GRID_W
    row = jnp.broadcast_to(jnp.arange(rows, dtype=jnp.float32)[:, None], (rows, GRID_W)).reshape(-1)
    col = jnp.broadcast_to(jnp.arange(GRID_W, dtype=jnp.float32)[None, :], (rows, GRID_W)).reshape(-1)
    axis_dim = rot_dim // 2
    inv_freq = ROPE_THETA ** (-jnp.arange(0, axis_dim, 2, dtype=jnp.float32) / axis_dim)
    ang_row = row[:, None] * inv_freq[None, :]
    ang_col = col[:, None] * inv_freq[None, :]
    return (jnp.cos(ang_row), jnp.sin(ang_row), jnp.cos(ang_col), jnp.sin(ang_col))


def rotate_axis(x, cos, sin):
    x1, x2 = jnp.split(x, 2, axis=-1)
    c = cos[:, None, :]
    s = sin[:, None, :]
    return jnp.concatenate([x1 * c - x2 * s, x2 * c + x1 * s], axis=-1)


def apply_grid_rope(x, tables):
    cr, sr, cc, sc = tables
    half = x.shape[-1] // 2
    xf = x.astype(jnp.float32)
    out = jnp.concatenate([rotate_axis(xf[..., :half], cr, sr), rotate_axis(xf[..., half:], cc, sc)], axis=-1)
    return out.astype(x.dtype)


def block_attention(q, k, v):
    b, sq, kvh, g, dk = q.shape
    nb = sq // Q_BLOCK
    scale = dk ** -0.5
    qb = jnp.moveaxis(q.reshape(b, nb, Q_BLOCK, kvh, g, dk), 1, 0)

    def one_block(qi):
        s = jnp.einsum('bqhgd,bkhd->bhgqk', qi, k, preferred_element_type=jnp.float32) * scale
        p = jax.nn.softmax(s, axis=-1).astype(v.dtype)
        return jnp.einsum('bhgqk,bkhe->bqhge', p, v)

    o = lax.map(one_block, qb)
    return jnp.moveaxis(o, 0, 1).reshape(b, sq, kvh * g * v.shape[-1])


def gqa_mixer(h_lat, h_ctx, w_qkv, q_norm, k_norm, w_o, rope, need_ctx):
    def project(h, rotate):
        b, n, _ = h.shape
        q, k, v = jnp.split(h @ w_qkv, [GQA_HEADS * GQA_HEAD_DIM, (GQA_HEADS + GQA_KV_HEADS) * GQA_HEAD_DIM], axis=-1)
        q = rms_norm(q.reshape(b, n, GQA_HEADS, GQA_HEAD_DIM), q_norm)
        k = rms_norm(k.reshape(b, n, GQA_KV_HEADS, GQA_HEAD_DIM), k_norm)
        v = v.reshape(b, n, GQA_KV_HEADS, GQA_HEAD_DIM)
        if rotate:
            q = apply_grid_rope(q, rope)
            k = apply_grid_rope(k, rope)
        return q.reshape(b, n, GQA_KV_HEADS, GQA_GROUP, GQA_HEAD_DIM), k, v

    q_lat, k_lat, v_lat = project(h_lat, True)
    q_ctx, k_ctx, v_ctx = project(h_ctx, False)
    k_all = jnp.concatenate([k_ctx, k_lat], axis=1)
    v_all = jnp.concatenate([v_ctx, v_lat], axis=1)
    y_lat = block_attention(q_lat, k_all, v_all) @ w_o
    y_ctx = block_attention(q_ctx, k_ctx, v_ctx) @ w_o if need_ctx else None
    return y_lat, y_ctx


def mla_mixer(h_lat, h_ctx, w_in, q_norm, kv_norm, w_uq, w_ukv, w_o, rope, need_ctx):
    def project(h, rotate):
        b, n, _ = h.shape
        cq, ckv, k_pe = jnp.split(h @ w_in, [MLA_Q_RANK, MLA_Q_RANK + MLA_KV_RANK], axis=-1)
        q = (rms_norm(cq, q_norm) @ w_uq).reshape(b, n, MLA_HEADS, MLA_NOPE_DIM + MLA_ROPE_DIM)
        q_nope, q_pe = jnp.split(q, [MLA_NOPE_DIM], axis=-1)
        kv = (rms_norm(ckv, kv_norm) @ w_ukv).reshape(b, n, MLA_HEADS, MLA_NOPE_DIM + MLA_V_DIM)
        k_nope, v = jnp.split(kv, [MLA_NOPE_DIM], axis=-1)
        k_pe = k_pe[:, :, None, :]
        if rotate:
            q_pe = apply_grid_rope(q_pe, rope)
            k_pe = apply_grid_rope(k_pe, rope)
        q = jnp.concatenate([q_nope, q_pe], axis=-1)[:, :, :, None, :]
        k = jnp.concatenate([k_nope, jnp.broadcast_to(k_pe, (b, n, MLA_HEADS, MLA_ROPE_DIM))], axis=-1)
        return q, k, v

    q_lat, k_lat, v_lat = project(h_lat, True)
    q_ctx, k_ctx, v_ctx = project(h_ctx, False)
    k_all = jnp.concatenate([k_ctx, k_lat], axis=1)
    v_all = jnp.concatenate([v_ctx, v_lat], axis=1)
    y_lat = block_attention(q_lat, k_all, v_all) @ w_o
    y_ctx = block_attention(q_ctx, k_ctx, v_ctx) @ w_o if need_ctx else None
    return y_lat, y_ctx


def sq_relu_mlp(h, w1, w2):
    return jnp.square(jax.nn.relu(h @ w1)) @ w2


def setup_inputs(seed: int = 0) -> dict:
    key = jax.random.key(seed)
    ks = jax.random.split(key, 20)
    d = D_MODEL
    n_a = (DEPTH + N_MIXERS - 1) // N_MIXERS
    n_b = DEPTH // N_MIXERS

    def w(k, shape, fan_in, gain=1.0):
        return jax.random.normal(k, shape, jnp.float32) * (gain * fan_in ** -0.5)

    def gain_vec(k, shape):
        return 1.0 + 0.02 * jax.random.normal(k, shape, jnp.float32)

    qkv_cols = (GQA_HEADS + 2 * GQA_KV_HEADS) * GQA_HEAD_DIM
    mla_in_cols = MLA_Q_RANK + MLA_KV_RANK + MLA_ROPE_DIM
    return {
        "x": jax.random.normal(ks[0], (BATCH, SEQ, d), jnp.float32),
        "c": jax.random.normal(ks[1], (BATCH, d), jnp.float32),
        "ctx": jax.random.normal(ks[2], (BATCH, CTX_LEN, d), jnp.float32),
        "c_ctx": jax.random.normal(ks[3], (d,), jnp.float32),
        "w_ada": w(ks[4], (DEPTH, d, 6 * d), d),
        "b_ada": 0.02 * jax.random.normal(ks[5], (DEPTH, 6 * d), jnp.float32),
        "ln_g": gain_vec(ks[6], (DEPTH, 2, d)),
        "ln_b": 0.02 * jax.random.normal(ks[7], (DEPTH, 2, d), jnp.float32),
        "mlp_w1": w(ks[8], (DEPTH, d, FFN_HIDDEN), d),
        "mlp_w2": w(ks[9], (DEPTH, FFN_HIDDEN, d), FFN_HIDDEN, DEEPNORM_BETA),
        "gqa_w_qkv": w(ks[10], (n_a, d, qkv_cols), d),
        "gqa_q_norm": gain_vec(ks[11], (n_a, GQA_HEAD_DIM)),
        "gqa_k_norm": gain_vec(ks[12], (n_a, GQA_HEAD_DIM)),
        "gqa_w_o": w(ks[13], (n_a, GQA_HEADS * GQA_HEAD_DIM, d), GQA_HEADS * GQA_HEAD_DIM, DEEPNORM_BETA),
        "mla_w_in": w(ks[14], (n_b, d, mla_in_cols), d),
        "mla_q_norm": gain_vec(ks[15], (n_b, MLA_Q_RANK)),
        "mla_kv_norm": gain_vec(ks[16], (n_b, MLA_KV_RANK)),
        "mla_w_uq": w(ks[17], (n_b, MLA_Q_RANK, MLA_HEADS * (MLA_NOPE_DIM + MLA_ROPE_DIM)), MLA_Q_RANK),
        "mla_w_ukv": w(ks[18], (n_b, MLA_KV_RANK, MLA_HEADS * (MLA_NOPE_DIM + MLA_V_DIM)), MLA_KV_RANK),
        "mla_w_o": w(ks[19], (n_b, MLA_HEADS * MLA_V_DIM, d), MLA_HEADS * MLA_V_DIM, DEEPNORM_BETA),
    }


def reference(x, c, ctx, c_ctx, w_ada, b_ada, ln_g, ln_b, mlp_w1, mlp_w2,
              gqa_w_qkv, gqa_q_norm, gqa_k_norm, gqa_w_o,
              mla_w_in, mla_q_norm, mla_kv_norm, mla_w_uq, mla_w_ukv, mla_w_o):
    n_tok = x.shape[1]
    rope_gqa = grid_rope_tables(n_tok, GQA_HEAD_DIM)
    rope_mla = grid_rope_tables(n_tok, MLA_ROPE_DIM)
    alpha = DEEPNORM_ALPHA
    silu_c = jax.nn.silu(c)
    silu_cc = jax.nn.silu(c_ctx)
    xc = ctx
    for i in range(DEPTH):
        need_ctx = i < DEPTH - 1
        mod_lat = (silu_c @ w_ada[i] + b_ada[i])[:, None, :]
        mod_ctx = silu_cc @ w_ada[i] + b_ada[i]
        sh_a, sc_a, g_a, sh_m, sc_m, g_m = jnp.split(mod_lat, 6, axis=-1)
        csh_a, csc_a, cg_a, csh_m, csc_m, cg_m = jnp.split(mod_ctx, 6, axis=-1)

        h_lat = x * (1.0 + sc_a) + sh_a
        h_ctx = xc * (1.0 + csc_a) + csh_a
        j = i // N_MIXERS
        if i % N_MIXERS == 0:
            y_lat, y_ctx = gqa_mixer(h_lat, h_ctx, gqa_w_qkv[j], gqa_q_norm[j], gqa_k_norm[j],
                                     gqa_w_o[j], rope_gqa, need_ctx)
        else:
            y_lat, y_ctx = mla_mixer(h_lat, h_ctx, mla_w_in[j], mla_q_norm[j], mla_kv_norm[j],
                                     mla_w_uq[j], mla_w_ukv[j], mla_w_o[j], rope_mla, need_ctx)

        x = layer_norm(alpha * x + g_a * y_lat, ln_g[i, 0], ln_b[i, 0])
        x = layer_norm(alpha * x + g_m * sq_relu_mlp(x * (1.0 + sc_m) + sh_m, mlp_w1[i], mlp_w2[i]),
                       ln_g[i, 1], ln_b[i, 1])
        if need_ctx:
            xc = layer_norm(alpha * xc + cg_a * y_ctx, ln_g[i, 0], ln_b[i, 0])
            xc = layer_norm(alpha * xc + cg_m * sq_relu_mlp(xc * (1.0 + csc_m) + csh_m, mlp_w1[i], mlp_w2[i]),
                            ln_g[i, 1], ln_b[i, 1])
    return x
```

```python
import functools
import math

import jax
import jax.numpy as jnp
from jax import lax
from jax.experimental import pallas as pl
from jax.experimental.pallas import tpu as pltpu

D_MODEL = 1024
DEPTH = 4
GRID_W = 64
N_MIXERS = 2
GQA_HEADS = 16
GQA_KV_HEADS = 4
GQA_HEAD_DIM = 64
MLA_HEADS = 16
MLA_Q_RANK = 384
MLA_KV_RANK = 256
MLA_NOPE_DIM = 64
MLA_ROPE_DIM = 32
MLA_V_DIM = 64
MLA_QK_DIM = MLA_NOPE_DIM + MLA_ROPE_DIM
MLA_QK_PAD = 128
FFN_HIDDEN = 4 * D_MODEL
ROPE_THETA = 10000.0
NORM_EPS = 1e-6
DEEPNORM_ALPHA = (2.0 * DEPTH) ** 0.25
LOG2E = 1.4426950408889634

LAT_TILE = 512
KV_CHUNK = 256
FFN_CHUNK = 1024
MOD_ROWS = 16
VMEM_LIMIT_BYTES = 56 * 1024 * 1024

F32 = jnp.float32
BF16 = jnp.bfloat16


def _params(n_axes):
    return pltpu.CompilerParams(dimension_semantics=("arbitrary",) * n_axes,
                                vmem_limit_bytes=VMEM_LIMIT_BYTES)


def _mod_kernel(cv_ref, w_ref, b_ref, o_ref):
    c = cv_ref[...]
    s = c * jax.nn.sigmoid(c)
    o_ref[0] = jnp.dot(s, w_ref[0], precision=lax.Precision.HIGHEST,
                       preferred_element_type=F32) + b_ref[0]


def _modulation(cv, w_ada, b_ada):
    d = D_MODEL
    return pl.pallas_call(
        _mod_kernel,
        grid=(DEPTH, 6),
        in_specs=[pl.BlockSpec((MOD_ROWS, d), lambda i, j: (0, 0)),
                  pl.BlockSpec((1, d, d), lambda i, j: (i, 0, j)),
                  pl.BlockSpec((1, 1, d), lambda i, j: (i, 0, j))],
        out_specs=pl.BlockSpec((1, MOD_ROWS, d), lambda i, j: (i, 0, j)),
        out_shape=jax.ShapeDtypeStruct((DEPTH, MOD_ROWS, 6 * d), F32),
        compiler_params=_params(2),
        name="adaln_modulation",
    )(cv, w_ada, b_ada.reshape(DEPTH, 1, 6 * d))


def _modulate(x, mod_ref):
    d = D_MODEL
    shift = mod_ref[0, 0, 0:d, :]
    scale = mod_ref[0, 0, d:2 * d, :]
    return x * (1.0 + scale) + shift


def _rms_rows(x, gain_col):
    ms = jnp.mean(x * x, axis=0, keepdims=True)
    return x * lax.rsqrt(ms + NORM_EPS) * gain_col


def _layer_norm_rows(z, g_col, b_col):
    mu = jnp.mean(z, axis=0, keepdims=True)
    zc = z - mu
    var = jnp.mean(zc * zc, axis=0, keepdims=True)
    return zc * lax.rsqrt(var + NORM_EPS) * g_col + b_col


def _rotate_rows(x, rope_ref):
    q = x.shape[0] // 4
    cr, sr, cc, sc = rope_ref[0], rope_ref[1], rope_ref[2], rope_ref[3]
    a1, a2, b1, b2 = x[0:q], x[q:2 * q], x[2 * q:3 * q], x[3 * q:4 * q]
    return jnp.concatenate([a1 * cr - a2 * sr, a2 * cr + a1 * sr,
                            b1 * cc - b2 * sc, b2 * cc + b1 * sc], axis=0)


def _gqa_proj_kernel(x_ref, mod_ref, w_ref, gq_ref, gk_ref, *rest, rotate):
    if rotate:
        rope_ref, q_ref, k_ref, v_ref = rest
    else:
        q_ref, k_ref, v_ref = rest
    hd = GQA_HEAD_DIM
    nq = GQA_HEADS * hd
    nk = GQA_KV_HEADS * hd
    h = _modulate(x_ref[0], mod_ref).astype(BF16)
    p = jnp.dot(w_ref[...], h, preferred_element_type=F32)
    gq = gq_ref[...]
    gk = gk_ref[...]
    q_scale = hd ** -0.5 * LOG2E
    for i in range(GQA_HEADS):
        qh = _rms_rows(p[i * hd:(i + 1) * hd], gq)
        if rotate:
            qh = _rotate_rows(qh, rope_ref)
        q_ref[0, i * hd:(i + 1) * hd, :] = (qh * q_scale).astype(BF16)
    ks = []
    for j in range(GQA_KV_HEADS):
        kh = _rms_rows(p[nq + j * hd:nq + (j + 1) * hd], gk)
        if rotate:
            kh = _rotate_rows(kh, rope_ref)
        ks.append(kh)
    kt = jnp.concatenate(ks, axis=0).T
    for j in range(GQA_KV_HEADS):
        k_ref[0, j] = kt[:, j * hd:(j + 1) * hd].astype(BF16)
    v_ref[0] = p[nq + nk:nq + 2 * nk].astype(BF16)


def _gqa_proj(x_t, modc, layer, mod_index, w_t, gq_col, gk_col, rope, tile):
    b, d, n = x_t.shape
    hd = GQA_HEAD_DIM
    nq, nk = GQA_HEADS * hd, GQA_KV_HEADS * hd
    rotate = rope is not None
    in_specs = [pl.BlockSpec((1, d, tile), lambda bi, ti: (bi, 0, ti)),
                pl.BlockSpec((1, 1, 2 * d, 1), lambda bi, ti: (layer, mod_index(bi), 0, 0)),
                pl.BlockSpec(w_t.shape, lambda bi, ti: (0, 0)),
                pl.BlockSpec(gq_col.shape, lambda bi, ti: (0, 0)),
                pl.BlockSpec(gk_col.shape, lambda bi, ti: (0, 0))]
    args = [x_t, modc, w_t, gq_col, gk_col]
    if rotate:
        in_specs.append(pl.BlockSpec((4, hd // 4, tile), lambda bi, ti: (0, 0, ti)))
        args.append(rope)
    return pl.pallas_call(
        functools.partial(_gqa_proj_kernel, rotate=rotate),
        grid=(b, n // tile),
        in_specs=in_specs,
        out_specs=[pl.BlockSpec((1, nq, tile), lambda bi, ti: (bi, 0, ti)),
                   pl.BlockSpec((1, GQA_KV_HEADS, tile, hd), lambda bi, ti: (bi, 0, ti, 0)),
                   pl.BlockSpec((1, nk, tile), lambda bi, ti: (bi, 0, ti))],
        out_shape=[jax.ShapeDtypeStruct((b, nq, n), BF16),
                   jax.ShapeDtypeStruct((b, GQA_KV_HEADS, n, hd), BF16),
                   jax.ShapeDtypeStruct((b, nk, n), BF16)],
        compiler_params=_params(2),
        name="gqa_proj_lat" if rotate else "gqa_proj_ctx",
    )(*args)


def _mla_proj_kernel(x_ref, mod_ref, win_ref, gq_ref, gkv_ref, wuq_ref, wukv_ref, *rest, rotate):
    if rotate:
        rope_ref, q_ref, k_ref, v_ref = rest
    else:
        q_ref, k_ref, v_ref = rest
    nh, nope, rd, pad = MLA_HEADS, MLA_NOPE_DIM, MLA_ROPE_DIM, MLA_QK_PAD
    t = x_ref.shape[2]
    h = _modulate(x_ref[0], mod_ref).astype(BF16)
    p = jnp.dot(win_ref[...], h, preferred_element_type=F32)
    cq = _rms_rows(p[0:MLA_Q_RANK], gq_ref[...]).astype(BF16)
    ckv = _rms_rows(p[MLA_Q_RANK:MLA_Q_RANK + MLA_KV_RANK], gkv_ref[...]).astype(BF16)
    k_pe = p[MLA_Q_RANK + MLA_KV_RANK:MLA_Q_RANK + MLA_KV_RANK + rd]
    if rotate:
        k_pe = _rotate_rows(k_pe, rope_ref)
    q = jnp.dot(wuq_ref[...], cq, preferred_element_type=F32)
    kv = jnp.dot(wukv_ref[...], ckv, preferred_element_type=F32)
    q_scale = MLA_QK_DIM ** -0.5 * LOG2E
    zeros = jnp.zeros((pad - nope - rd, t), F32)
    for i in range(nh):
        qh = q[i * pad:(i + 1) * pad]
        q_pe = qh[nope:nope + rd]
        if rotate:
            q_pe = _rotate_rows(q_pe, rope_ref)
        qh = jnp.concatenate([qh[0:nope], q_pe, qh[nope + rd:pad]], axis=0)
        q_ref[0, i * pad:(i + 1) * pad, :] = (qh * q_scale).astype(BF16)
        kh = jnp.concatenate([kv[i * nope:(i + 1) * nope], k_pe, zeros], axis=0)
        k_ref[0, i] = kh.T.astype(BF16)
    v_ref[0] = kv[nh * nope:].astype(BF16)


def _mla_proj(x_t, modc, layer, mod_index, win_t, gq_col, gkv_col, wuq_t, wukv_t, rope, tile):
    b, d, n = x_t.shape
    nh, pad = MLA_HEADS, MLA_QK_PAD
    rotate = rope is not None
    in_specs = [pl.BlockSpec((1, d, tile), lambda bi, ti: (bi, 0, ti)),
                pl.BlockSpec((1, 1, 2 * d, 1), lambda bi, ti: (layer, mod_index(bi), 0, 0)),
                pl.BlockSpec(win_t.shape, lambda bi, ti: (0, 0)),
                pl.BlockSpec(gq_col.shape, lambda bi, ti: (0, 0)),
                pl.BlockSpec(gkv_col.shape, lambda bi, ti: (0, 0)),
                pl.BlockSpec(wuq_t.shape, lambda bi, ti: (0, 0)),
                pl.BlockSpec(wukv_t.shape, lambda bi, ti: (0, 0))]
    args = [x_t, modc, win_t, gq_col, gkv_col, wuq_t, wukv_t]
    if rotate:
        in_specs.append(pl.BlockSpec((4, MLA_ROPE_DIM // 4, tile), lambda bi, ti: (0, 0, ti)))
        args.append(rope)
    return pl.pallas_call(
        functools.partial(_mla_proj_kernel, rotate=rotate),
        grid=(b, n // tile),
        in_specs=in_specs,
        out_specs=[pl.BlockSpec((1, nh * pad, tile), lambda bi, ti: (bi, 0, ti)),
                   pl.BlockSpec((1, nh, tile, pad), lambda bi, ti: (bi, 0, ti, 0)),
                   pl.BlockSpec((1, nh * MLA_V_DIM, tile), lambda bi, ti: (bi, 0, ti))],
        out_shape=[jax.ShapeDtypeStruct((b, nh * pad, n), BF16),
                   jax.ShapeDtypeStruct((b, nh, n, pad), BF16),
                   jax.ShapeDtypeStruct((b, nh * MLA_V_DIM, n), BF16)],
        compiler_params=_params(2),
        name="mla_proj_lat" if rotate else "mla_proj_ctx",
    )(*args)


def _attn_kernel(q_ref, *rest, n_heads, group, dk_rows, dv, kv_lens):
    n_src = len(kv_lens)
    k_refs = rest[0:2 * n_src:2]
    v_refs = rest[1:2 * n_src:2]
    o_ref = rest[2 * n_src]
    tq = q_ref.shape[2]

    def head(hi, carry):
        kvh = hi // group
        q = q_ref[0, pl.ds(pl.multiple_of(hi * dk_rows, dk_rows), dk_rows), :]
        v_row = pl.multiple_of(kvh * dv, dv)
        m = jnp.full((1, tq), -jnp.inf, F32)
        l = jnp.zeros((1, tq), F32)
        acc = jnp.zeros((dv, tq), F32)
        for k_ref, v_ref, n_keys in zip(k_refs, v_refs, kv_lens):
            for c in range(n_keys // KV_CHUNK):
                kc = k_ref[0, kvh, c * KV_CHUNK:(c + 1) * KV_CHUNK, :]
                s = jnp.dot(kc, q, preferred_element_type=F32)
                m_new = jnp.maximum(m, jnp.max(s, axis=0, keepdims=True))
                alpha = jnp.exp2(m - m_new)
                p = jnp.exp2(s - m_new)
                l = alpha * l + jnp.sum(p, axis=0, keepdims=True)
                vc = v_ref[0, pl.ds(v_row, dv), c * KV_CHUNK:(c + 1) * KV_CHUNK]
                acc = alpha * acc + jnp.dot(vc, p.astype(BF16), preferred_element_type=F32)
                m = m_new
        o_ref[0, pl.ds(pl.multiple_of(hi * dv, dv), dv), :] = (acc / l).astype(BF16)
        return carry

    lax.fori_loop(0, n_heads, head, 0)


def _attention(q_t, kv_sources, n_heads, group, dk_rows, dv, tq, name):
    b, qrows, nq = q_t.shape
    in_specs = [pl.BlockSpec((1, qrows, tq), lambda bi, ti: (bi, 0, ti))]
    args = [q_t]
    kv_lens = []
    for k, v in kv_sources:
        in_specs.append(pl.BlockSpec((1,) + k.shape[1:], lambda bi, ti: (bi, 0, 0, 0)))
        in_specs.append(pl.BlockSpec((1,) + v.shape[1:], lambda bi, ti: (bi, 0, 0)))
        args += [k, v]
        kv_lens.append(k.shape[2])
    return pl.pallas_call(
        functools.partial(_attn_kernel, n_heads=n_heads, group=group, dk_rows=dk_rows, dv=dv,
                          kv_lens=tuple(kv_lens)),
        grid=(b, nq // tq),
        in_specs=in_specs,
        out_specs=pl.BlockSpec((1, n_heads * dv, tq), lambda bi, ti: (bi, 0, ti)),
        out_shape=jax.ShapeDtypeStruct((b, n_heads * dv, nq), BF16),
        compiler_params=_params(2),
        name=name,
    )(*args)


def _post_kernel(x_ref, a_ref, moda_ref, modm_ref, ln_ref, wo_ref, w1_ref, w2_ref, o_ref):
    d = D_MODEL
    x = x_ref[0]
    gate_a = moda_ref[0, 0, 0:d, :]
    shift_m = moda_ref[0, 0, d:2 * d, :]
    scale_m = modm_ref[0, 0, 0:d, :]
    gate_m = modm_ref[0, 0, d:2 * d, :]
    y = jnp.dot(wo_ref[...], a_ref[0], preferred_element_type=F32)
    x1 = _layer_norm_rows(DEEPNORM_ALPHA * x + gate_a * y, ln_ref[0, 0], ln_ref[0, 1])
    h = (x1 * (1.0 + scale_m) + shift_m).astype(BF16)
    acc = None
    for c in range(FFN_HIDDEN // FFN_CHUNK):
        u = jnp.dot(w1_ref[c * FFN_CHUNK:(c + 1) * FFN_CHUNK, :], h, preferred_element_type=F32)
        u = jnp.maximum(u, 0.0)
        u = (u * u).astype(BF16)
        part = jnp.dot(w2_ref[:, c * FFN_CHUNK:(c + 1) * FFN_CHUNK], u, preferred_element_type=F32)
        acc = part if acc is None else acc + part
    o_ref[0] = _layer_norm_rows(DEEPNORM_ALPHA * x1 + gate_m * acc, ln_ref[0, 2], ln_ref[0, 3])


def _post(x_t, attn_t, modc, ln_cols, layer, mod_index, wo_t, w1_t, w2_t, tile, name):
    b, d, n = x_t.shape
    const2 = lambda bi, ti: (0, 0)
    return pl.pallas_call(
        _post_kernel,
        grid=(b, n // tile),
        in_specs=[pl.BlockSpec((1, d, tile), lambda bi, ti: (bi, 0, ti)),
                  pl.BlockSpec((1, d, tile), lambda bi, ti: (bi, 0, ti)),
                  pl.BlockSpec((1, 1, 2 * d, 1), lambda bi, ti: (layer, mod_index(bi), 1, 0)),
                  pl.BlockSpec((1, 1, 2 * d, 1), lambda bi, ti: (layer, mod_index(bi), 2, 0)),
                  pl.BlockSpec((1, 4, d, 1), lambda bi, ti: (layer, 0, 0, 0)),
                  pl.BlockSpec(wo_t.shape, const2, pipeline_mode=pl.Buffered(1)),
                  pl.BlockSpec(w1_t.shape, const2, pipeline_mode=pl.Buffered(1)),
                  pl.BlockSpec(w2_t.shape, const2, pipeline_mode=pl.Buffered(1))],
        out_specs=pl.BlockSpec((1, d, tile), lambda bi, ti: (bi, 0, ti)),
        out_shape=jax.ShapeDtypeStruct((b, d, n), F32),
        compiler_params=_params(2),
        name=name,
    )(x_t, attn_t, modc, modc, ln_cols, wo_t, w1_t, w2_t)


def _rope_tables_t(n_tok, rot_dim):
    rows = n_tok // GRID_W
    row = jnp.broadcast_to(jnp.arange(rows, dtype=F32)[:, None], (rows, GRID_W)).reshape(-1)
    col = jnp.broadcast_to(jnp.arange(GRID_W, dtype=F32)[None, :], (rows, GRID_W)).reshape(-1)
    axis_dim = rot_dim // 2
    inv_freq = ROPE_THETA ** (-jnp.arange(0, axis_dim, 2, dtype=F32) / axis_dim)
    ang_row = inv_freq[:, None] * row[None, :]
    ang_col = inv_freq[:, None] * col[None, :]
    return jnp.stack([jnp.cos(ang_row), jnp.sin(ang_row), jnp.cos(ang_col), jnp.sin(ang_col)])


def _col(v):
    return v.astype(F32)[:, None]


def kernel(x, c, ctx, c_ctx, w_ada, b_ada, ln_g, ln_b, mlp_w1, mlp_w2, gqa_w_qkv, gqa_q_norm, gqa_k_norm, gqa_w_o, mla_w_in, mla_q_norm, mla_kv_norm, mla_w_uq, mla_w_ukv, mla_w_o):
    b, n_lat, d = x.shape
    n_ctx = ctx.shape[1]

    cv = jnp.concatenate([c, c_ctx[None, :], jnp.zeros((MOD_ROWS - b - 1, d), F32)], axis=0)
    mod = _modulation(cv, w_ada, b_ada)
    modc = mod[:, :b + 1, :, None]
    ln_cols = jnp.stack([ln_g[:, 0], ln_b[:, 0], ln_g[:, 1], ln_b[:, 1]], axis=1)[..., None]

    lat_mod = lambda bi: bi
    ctx_mod = lambda bi: b

    rope_gqa = _rope_tables_t(n_lat, GQA_HEAD_DIM)
    rope_mla = _rope_tables_t(n_lat, MLA_ROPE_DIM)

    x_t = jnp.swapaxes(x, 1, 2)
    xc_t = jnp.swapaxes(ctx, 1, 2)

    for i in range(DEPTH):
        need_ctx = i < DEPTH - 1
        j = i // N_MIXERS
        w1_t = mlp_w1[i].T.astype(BF16)
        w2_t = mlp_w2[i].T.astype(BF16)
        if i % N_MIXERS == 0:
            w_t = gqa_w_qkv[j].T.astype(BF16)
            wo_t = gqa_w_o[j].T.astype(BF16)
            gq, gk = _col(gqa_q_norm[j]), _col(gqa_k_norm[j])
            q_l, k_l, v_l = _gqa_proj(x_t, modc, i, lat_mod, w_t, gq, gk, rope_gqa, LAT_TILE)
            q_c, k_c, v_c = _gqa_proj(xc_t, modc, i, ctx_mod, w_t, gq, gk, None, n_ctx)
            heads, group, dk_rows, dv = GQA_HEADS, GQA_HEADS // GQA_KV_HEADS, GQA_HEAD_DIM, GQA_HEAD_DIM
        else:
            win_t = mla_w_in[j].T.astype(BF16)
            wuq = mla_w_uq[j].reshape(MLA_Q_RANK, MLA_HEADS, MLA_QK_DIM)
            wuq = jnp.pad(wuq, ((0, 0), (0, 0), (0, MLA_QK_PAD - MLA_QK_DIM)))
            wuq_t = wuq.reshape(MLA_Q_RANK, MLA_HEADS * MLA_QK_PAD).T.astype(BF16)
            wukv = mla_w_ukv[j].reshape(MLA_KV_RANK, MLA_HEADS, MLA_NOPE_DIM + MLA_V_DIM)
            wukv = jnp.concatenate([wukv[:, :, :MLA_NOPE_DIM].reshape(MLA_KV_RANK, -1),
                                    wukv[:, :, MLA_NOPE_DIM:].reshape(MLA_KV_RANK, -1)], axis=1)
            wukv_t = wukv.T.astype(BF16)
            wo_t = mla_w_o[j].T.astype(BF16)
            gq, gkv = _col(mla_q_norm[j]), _col(mla_kv_norm[j])
            q_l, k_l, v_l = _mla_proj(x_t, modc, i, lat_mod, win_t, gq, gkv, wuq_t, wukv_t, rope_mla, LAT_TILE)
            q_c, k_c, v_c = _mla_proj(xc_t, modc, i, ctx_mod, win_t, gq, gkv, wuq_t, wukv_t, None, n_ctx)
            heads, group, dk_rows, dv = MLA_HEADS, 1, MLA_QK_PAD, MLA_V_DIM

        a_l = _attention(q_l, [(k_c, v_c), (k_l, v_l)], heads, group, dk_rows, dv, LAT_TILE, "attn_lat")
        x_t = _post(x_t, a_l, modc, ln_cols, i, lat_mod, wo_t, w1_t, w2_t, LAT_TILE, "post_lat")
        if need_ctx:
            a_c = _attention(q_c, [(k_c, v_c)], heads, group, dk_rows, dv, n_ctx, "attn_ctx")
            xc_t = _post(xc_t, a_c, modc, ln_cols, i, ctx_mod, wo_t, w1_t, w2_t, n_ctx, "post_ctx")

    return jnp.swapaxes(x_t, 1, 2)
```

```python
import functools
import math

import jax
import jax.numpy as jnp
from jax import lax
from jax.experimental import pallas as pl
from jax.experimental.pallas import tpu as pltpu

D_MODEL = 1024
DEPTH = 4
GRID_W = 64
N_MIXERS = 2
GQA_HEADS = 16
GQA_KV_HEADS = 4
GQA_HEAD_DIM = 64
MLA_HEADS = 16
MLA_Q_RANK = 384
MLA_KV_RANK = 256
MLA_NOPE_DIM = 64
MLA_ROPE_DIM = 32
MLA_V_DIM = 64
MLA_QK_DIM = MLA_NOPE_DIM + MLA_ROPE_DIM
MLA_QK_PAD = 128
FFN_HIDDEN = 4 * D_MODEL
ROPE_THETA = 10000.0
NORM_EPS = 1e-6
DEEPNORM_ALPHA = (2.0 * DEPTH) ** 0.25
LOG2E = 1.4426950408889634

HEAD_V_DIM = 64
ONES_ROWS = 16
V_ROWS = HEAD_V_DIM + ONES_ROWS

LAT_TILE = 512
FFN_CHUNK = 1024
MOD_ROWS = 16
VMEM_LIMIT_BYTES = 56 * 1024 * 1024

F32 = jnp.float32
BF16 = jnp.bfloat16


def _params(n_axes):
    return pltpu.CompilerParams(dimension_semantics=("arbitrary",) * n_axes,
                                vmem_limit_bytes=VMEM_LIMIT_BYTES)


def _mod_kernel(cv_ref, w_ref, b_ref, o_ref):
    c = cv_ref[...]
    s = c * jax.nn.sigmoid(c)
    o_ref[0] = jnp.dot(s, w_ref[0], precision=lax.Precision.HIGHEST,
                       preferred_element_type=F32) + b_ref[0]


def _modulation(cv, w_ada, b_ada):
    d = D_MODEL
    return pl.pallas_call(
        _mod_kernel,
        grid=(DEPTH, 6),
        in_specs=[pl.BlockSpec((MOD_ROWS, d), lambda i, j: (0, 0)),
                  pl.BlockSpec((1, d, d), lambda i, j: (i, 0, j)),
                  pl.BlockSpec((1, 1, d), lambda i, j: (i, 0, j))],
        out_specs=pl.BlockSpec((1, MOD_ROWS, d), lambda i, j: (i, 0, j)),
        out_shape=jax.ShapeDtypeStruct((DEPTH, MOD_ROWS, 6 * d), F32),
        compiler_params=_params(2),
        name="adaln_modulation",
    )(cv, w_ada, b_ada.reshape(DEPTH, 1, 6 * d))


def _modulate(x, mod_ref):
    d = D_MODEL
    shift = mod_ref[0, 0, 0:d, :]
    scale = mod_ref[0, 0, d:2 * d, :]
    return x * (1.0 + scale) + shift


def _rms_rows(x, gain_col):
    ms = jnp.mean(x * x, axis=0, keepdims=True)
    return x * lax.rsqrt(ms + NORM_EPS) * gain_col


def _layer_norm_rows(z, g_col, b_col):
    mu = jnp.mean(z, axis=0, keepdims=True)
    zc = z - mu
    var = jnp.mean(zc * zc, axis=0, keepdims=True)
    return zc * lax.rsqrt(var + NORM_EPS) * g_col + b_col


def _rotate_rows(x, rope_ref):
    q = x.shape[0] // 4
    cr, sr, cc, sc = rope_ref[0], rope_ref[1], rope_ref[2], rope_ref[3]
    a1, a2, b1, b2 = x[0:q], x[q:2 * q], x[2 * q:3 * q], x[3 * q:4 * q]
    return jnp.concatenate([a1 * cr - a2 * sr, a2 * cr + a1 * sr,
                            b1 * cc - b2 * sc, b2 * cc + b1 * sc], axis=0)


def _store_values(v_ref, v, n_heads):
    ones = jnp.ones((ONES_ROWS, v.shape[1]), BF16)
    for j in range(n_heads):
        v_ref[0, j * V_ROWS:j * V_ROWS + HEAD_V_DIM, :] = v[j * HEAD_V_DIM:(j + 1) * HEAD_V_DIM].astype(BF16)
        v_ref[0, j * V_ROWS + HEAD_V_DIM:(j + 1) * V_ROWS, :] = ones


def _gqa_proj_kernel(x_ref, mod_ref, w_ref, gq_ref, gk_ref, *rest, rotate):
    if rotate:
        rope_ref, q_ref, k_ref, v_ref = rest
    else:
        q_ref, k_ref, v_ref = rest
    hd = GQA_HEAD_DIM
    nq = GQA_HEADS * hd
    nk = GQA_KV_HEADS * hd
    h = _modulate(x_ref[0], mod_ref).astype(BF16)
    p = jnp.dot(w_ref[...], h, preferred_element_type=F32)
    gq = gq_ref[...]
    gk = gk_ref[...]
    q_scale = hd ** -0.5 * LOG2E
    for i in range(GQA_HEADS):
        qh = _rms_rows(p[i * hd:(i + 1) * hd], gq)
        if rotate:
            qh = _rotate_rows(qh, rope_ref)
        q_ref[0, i * hd:(i + 1) * hd, :] = (qh * q_scale).astype(BF16)
    ks = []
    for j in range(GQA_KV_HEADS):
        kh = _rms_rows(p[nq + j * hd:nq + (j + 1) * hd], gk)
        if rotate:
            kh = _rotate_rows(kh, rope_ref)
        ks.append(kh)
    kt = jnp.concatenate(ks, axis=0).T
    for j in range(GQA_KV_HEADS):
        k_ref[0, j] = kt[:, j * hd:(j + 1) * hd].astype(BF16)
    _store_values(v_ref, p[nq + nk:nq + 2 * nk], GQA_KV_HEADS)


def _gqa_proj(x_t, modc, layer, mod_index, w_t, gq_col, gk_col, rope, tile):
    b, d, n = x_t.shape
    hd = GQA_HEAD_DIM
    nq, nk = GQA_HEADS * hd, GQA_KV_HEADS * hd
    rotate = rope is not None
    in_specs = [pl.BlockSpec((1, d, tile), lambda bi, ti: (bi, 0, ti)),
                pl.BlockSpec((1, 1, 2 * d, 1), lambda bi, ti: (layer, mod_index(bi), 0, 0)),
                pl.BlockSpec(w_t.shape, lambda bi, ti: (0, 0)),
                pl.BlockSpec(gq_col.shape, lambda bi, ti: (0, 0)),
                pl.BlockSpec(gk_col.shape, lambda bi, ti: (0, 0))]
    args = [x_t, modc, w_t, gq_col, gk_col]
    if rotate:
        in_specs.append(pl.BlockSpec((4, hd // 4, tile), lambda bi, ti: (0, 0, ti)))
        args.append(rope)
    return pl.pallas_call(
        functools.partial(_gqa_proj_kernel, rotate=rotate),
        grid=(b, n // tile),
        in_specs=in_specs,
        out_specs=[pl.BlockSpec((1, nq, tile), lambda bi, ti: (bi, 0, ti)),
                   pl.BlockSpec((1, GQA_KV_HEADS, tile, hd), lambda bi, ti: (bi, 0, ti, 0)),
                   pl.BlockSpec((1, GQA_KV_HEADS * V_ROWS, tile), lambda bi, ti: (bi, 0, ti))],
        out_shape=[jax.ShapeDtypeStruct((b, nq, n), BF16),
                   jax.ShapeDtypeStruct((b, GQA_KV_HEADS, n, hd), BF16),
                   jax.ShapeDtypeStruct((b, GQA_KV_HEADS * V_ROWS, n), BF16)],
        compiler_params=_params(2),
        name="gqa_proj_lat" if rotate else "gqa_proj_ctx",
    )(*args)


def _mla_proj_kernel(x_ref, mod_ref, win_ref, gq_ref, gkv_ref, wuq_ref, wukv_ref, *rest, rotate):
    if rotate:
        rope_ref, q_ref, k_ref, v_ref = rest
    else:
        q_ref, k_ref, v_ref = rest
    nh, nope, rd, pad = MLA_HEADS, MLA_NOPE_DIM, MLA_ROPE_DIM, MLA_QK_PAD
    t = x_ref.shape[2]
    h = _modulate(x_ref[0], mod_ref).astype(BF16)
    p = jnp.dot(win_ref[...], h, preferred_element_type=F32)
    cq = _rms_rows(p[0:MLA_Q_RANK], gq_ref[...]).astype(BF16)
    ckv = _rms_rows(p[MLA_Q_RANK:MLA_Q_RANK + MLA_KV_RANK], gkv_ref[...]).astype(BF16)
    k_pe = p[MLA_Q_RANK + MLA_KV_RANK:MLA_Q_RANK + MLA_KV_RANK + rd]
    if rotate:
        k_pe = _rotate_rows(k_pe, rope_ref)
    q = jnp.dot(wuq_ref[...], cq, preferred_element_type=F32)
    kv = jnp.dot(wukv_ref[...], ckv, preferred_element_type=F32)
    q_scale = MLA_QK_DIM ** -0.5 * LOG2E
    zeros = jnp.zeros((pad - nope - rd, t), F32)
    for i in range(nh):
        qh = q[i * pad:(i + 1) * pad]
        q_pe = qh[nope:nope + rd]
        if rotate:
            q_pe = _rotate_rows(q_pe, rope_ref)
        qh = jnp.concatenate([qh[0:nope], q_pe, qh[nope + rd:pad]], axis=0)
        q_ref[0, i * pad:(i + 1) * pad, :] = (qh * q_scale).astype(BF16)
        kh = jnp.concatenate([kv[i * nope:(i + 1) * nope], k_pe, zeros], axis=0)
        k_ref[0, i] = kh.T.astype(BF16)
    _store_values(v_ref, kv[nh * nope:], nh)


def _mla_proj(x_t, modc, layer, mod_index, win_t, gq_col, gkv_col, wuq_t, wukv_t, rope, tile):
    b, d, n = x_t.shape
    nh, pad = MLA_HEADS, MLA_QK_PAD
    rotate = rope is not None
    in_specs = [pl.BlockSpec((1, d, tile), lambda bi, ti: (bi, 0, ti)),
                pl.BlockSpec((1, 1, 2 * d, 1), lambda bi, ti: (layer, mod_index(bi), 0, 0)),
                pl.BlockSpec(win_t.shape, lambda bi, ti: (0, 0)),
                pl.BlockSpec(gq_col.shape, lambda bi, ti: (0, 0)),
                pl.BlockSpec(gkv_col.shape, lambda bi, ti: (0, 0)),
                pl.BlockSpec(wuq_t.shape, lambda bi, ti: (0, 0)),
                pl.BlockSpec(wukv_t.shape, lambda bi, ti: (0, 0))]
    args = [x_t, modc, win_t, gq_col, gkv_col, wuq_t, wukv_t]
    if rotate:
        in_specs.append(pl.BlockSpec((4, MLA_ROPE_DIM // 4, tile), lambda bi, ti: (0, 0, ti)))
        args.append(rope)
    return pl.pallas_call(
        functools.partial(_mla_proj_kernel, rotate=rotate),
        grid=(b, n // tile),
        in_specs=in_specs,
        out_specs=[pl.BlockSpec((1, nh * pad, tile), lambda bi, ti: (bi, 0, ti)),
                   pl.BlockSpec((1, nh, tile, pad), lambda bi, ti: (bi, 0, ti, 0)),
                   pl.BlockSpec((1, nh * V_ROWS, tile), lambda bi, ti: (bi, 0, ti))],
        out_shape=[jax.ShapeDtypeStruct((b, nh * pad, n), BF16),
                   jax.ShapeDtypeStruct((b, nh, n, pad), BF16),
                   jax.ShapeDtypeStruct((b, nh * V_ROWS, n), BF16)],
        compiler_params=_params(2),
        name="mla_proj_lat" if rotate else "mla_proj_ctx",
    )(*args)


def _attn_kernel(q_ref, *rest, n_heads, group, dk_rows, dv):
    n_src = (len(rest) - 2) // 2
    k_refs = rest[0:2 * n_src:2]
    v_refs = rest[1:2 * n_src:2]
    o_ref = rest[2 * n_src]
    s_ref = rest[2 * n_src + 1]
    bounds = [0]
    for k_ref in k_refs:
        bounds.append(bounds[-1] + k_ref.shape[2])

    def scores(hi, slot):
        kvh = hi // group
        q = q_ref[0, pl.ds(pl.multiple_of(hi * dk_rows, dk_rows), dk_rows), :]
        m = None
        for k_ref, lo, hi_key in zip(k_refs, bounds[:-1], bounds[1:]):
            s = jnp.dot(k_ref[0, kvh], q, preferred_element_type=F32)
            s_ref[slot, lo:hi_key, :] = s
            ms = jnp.max(s, axis=0, keepdims=True)
            m = ms if m is None else jnp.maximum(m, ms)
        return m

    def finish(hi, slot, m):
        kvh = hi // group
        v_row = pl.multiple_of(kvh * V_ROWS, ONES_ROWS)
        acc = None
        for v_ref, lo, hi_key in zip(v_refs, bounds[:-1], bounds[1:]):
            p = jnp.exp2(s_ref[slot, lo:hi_key, :] - m).astype(BF16)
            part = jnp.dot(v_ref[0, pl.ds(v_row, V_ROWS), :], p, preferred_element_type=F32)
            acc = part if acc is None else acc + part
        out = acc[0:dv] / acc[dv:dv + 1]
        o_ref[0, pl.ds(pl.multiple_of(hi * dv, dv), dv), :] = out.astype(BF16)

    def pair(i, m_even):
        h0 = 2 * i
        m_odd = scores(h0 + 1, 1)
        finish(h0, 0, m_even)
        m_next = scores(h0 + 2, 0)
        finish(h0 + 1, 1, m_odd)
        return m_next

    m_even = lax.fori_loop(0, n_heads // 2 - 1, pair, scores(0, 0))
    m_odd = scores(n_heads - 1, 1)
    finish(n_heads - 2, 0, m_even)
    finish(n_heads - 1, 1, m_odd)


def _attention(q_t, kv_sources, n_heads, group, dk_rows, dv, tq, name):
    b, qrows, nq = q_t.shape
    in_specs = [pl.BlockSpec((1, qrows, tq), lambda bi, ti: (bi, 0, ti))]
    args = [q_t]
    n_keys = 0
    for k, v in kv_sources:
        in_specs.append(pl.BlockSpec((1,) + k.shape[1:], lambda bi, ti: (bi, 0, 0, 0)))
        in_specs.append(pl.BlockSpec((1,) + v.shape[1:], lambda bi, ti: (bi, 0, 0)))
        args += [k, v]
        n_keys += k.shape[2]
    return pl.pallas_call(
        functools.partial(_attn_kernel, n_heads=n_heads, group=group, dk_rows=dk_rows, dv=dv),
        grid=(b, nq // tq),
        in_specs=in_specs,
        out_specs=pl.BlockSpec((1, n_heads * dv, tq), lambda bi, ti: (bi, 0, ti)),
        out_shape=jax.ShapeDtypeStruct((b, n_heads * dv, nq), BF16),
        scratch_shapes=[pltpu.VMEM((2, n_keys, tq), F32)],
        compiler_params=_params(2),
        name=name,
    )(*args)


def _post_kernel(x_ref, a_ref, moda_ref, modm_ref, ln_ref, wo_ref, w1_ref, w2_ref, o_ref):
    d = D_MODEL
    x = x_ref[0]
    gate_a = moda_ref[0, 0, 0:d, :]
    shift_m = moda_ref[0, 0, d:2 * d, :]
    scale_m = modm_ref[0, 0, 0:d, :]
    gate_m = modm_ref[0, 0, d:2 * d, :]
    y = jnp.dot(wo_ref[...], a_ref[0], preferred_element_type=F32)
    x1 = _layer_norm_rows(DEEPNORM_ALPHA * x + gate_a * y, ln_ref[0, 0], ln_ref[0, 1])
    h = (x1 * (1.0 + scale_m) + shift_m).astype(BF16)
    acc = None
    for c in range(FFN_HIDDEN // FFN_CHUNK):
        u = jnp.dot(w1_ref[c * FFN_CHUNK:(c + 1) * FFN_CHUNK, :], h, preferred_element_type=F32)
        u = jnp.maximum(u, 0.0)
        u = (u * u).astype(BF16)
        part = jnp.dot(w2_ref[:, c * FFN_CHUNK:(c + 1) * FFN_CHUNK], u, preferred_element_type=F32)
        acc = part if acc is None else acc + part
    o_ref[0] = _layer_norm_rows(DEEPNORM_ALPHA * x1 + gate_m * acc, ln_ref[0, 2], ln_ref[0, 3])


def _post(x_t, attn_t, modc, ln_cols, layer, mod_index, wo_t, w1_t, w2_t, tile, name):
    b, d, n = x_t.shape
    const2 = lambda bi, ti: (0, 0)
    return pl.pallas_call(
        _post_kernel,
        grid=(b, n // tile),
        in_specs=[pl.BlockSpec((1, d, tile), lambda bi, ti: (bi, 0, ti)),
                  pl.BlockSpec((1, d, tile), lambda bi, ti: (bi, 0, ti)),
                  pl.BlockSpec((1, 1, 2 * d, 1), lambda bi, ti: (layer, mod_index(bi), 1, 0)),
                  pl.BlockSpec((1, 1, 2 * d, 1), lambda bi, ti: (layer, mod_index(bi), 2, 0)),
                  pl.BlockSpec((1, 4, d, 1), lambda bi, ti: (layer, 0, 0, 0)),
                  pl.BlockSpec(wo_t.shape, const2, pipeline_mode=pl.Buffered(1)),
                  pl.BlockSpec(w1_t.shape, const2, pipeline_mode=pl.Buffered(1)),
                  pl.BlockSpec(w2_t.shape, const2, pipeline_mode=pl.Buffered(1))],
        out_specs=pl.BlockSpec((1, d, tile), lambda bi, ti: (bi, 0, ti)),
        out_shape=jax.ShapeDtypeStruct((b, d, n), F32),
        compiler_params=_params(2),
        name=name,
    )(x_t, attn_t, modc, modc, ln_cols, wo_t, w1_t, w2_t)


def _rope_tables_t(n_tok, rot_dim):
    rows = n_tok // GRID_W
    row = jnp.broadcast_to(jnp.arange(rows, dtype=F32)[:, None], (rows, GRID_W)).reshape(-1)
    col = jnp.broadcast_to(jnp.arange(GRID_W, dtype=F32)[None, :], (rows, GRID_W)).reshape(-1)
    axis_dim = rot_dim // 2
    inv_freq = ROPE_THETA ** (-jnp.arange(0, axis_dim, 2, dtype=F32) / axis_dim)
    ang_row = inv_freq[:, None] * row[None, :]
    ang_col = inv_freq[:, None] * col[None, :]
    return jnp.stack([jnp.cos(ang_row), jnp.sin(ang_row), jnp.cos(ang_col), jnp.sin(ang_col)])


def _col(v):
    return v.astype(F32)[:, None]


def kernel(x, c, ctx, c_ctx, w_ada, b_ada, ln_g, ln_b, mlp_w1, mlp_w2, gqa_w_qkv, gqa_q_norm, gqa_k_norm, gqa_w_o, mla_w_in, mla_q_norm, mla_kv_norm, mla_w_uq, mla_w_ukv, mla_w_o):
    b, n_lat, d = x.shape
    n_ctx = ctx.shape[1]

    cv = jnp.concatenate([c, c_ctx[None, :], jnp.zeros((MOD_ROWS - b - 1, d), F32)], axis=0)
    mod = _modulation(cv, w_ada, b_ada)
    modc = mod[:, :b + 1, :, None]
    ln_cols = jnp.stack([ln_g[:, 0], ln_b[:, 0], ln_g[:, 1], ln_b[:, 1]], axis=1)[..., None]

    lat_mod = lambda bi: bi
    ctx_mod = lambda bi: b

    rope_gqa = _rope_tables_t(n_lat, GQA_HEAD_DIM)
    rope_mla = _rope_tables_t(n_lat, MLA_ROPE_DIM)

    x_t = jnp.swapaxes(x, 1, 2)
    xc_t = jnp.swapaxes(ctx, 1, 2)

    for i in range(DEPTH):
        need_ctx = i < DEPTH - 1
        j = i // N_MIXERS
        w1_t = mlp_w1[i].T.astype(BF16)
        w2_t = mlp_w2[i].T.astype(BF16)
        if i % N_MIXERS == 0:
            w_t = gqa_w_qkv[j].T.astype(BF16)
            wo_t = gqa_w_o[j].T.astype(BF16)
            gq, gk = _col(gqa_q_norm[j]), _col(gqa_k_norm[j])
            q_l, k_l, v_l = _gqa_proj(x_t, modc, i, lat_mod, w_t, gq, gk, rope_gqa, LAT_TILE)
            q_c, k_c, v_c = _gqa_proj(xc_t, modc, i, ctx_mod, w_t, gq, gk, None, n_ctx)
            heads, group, dk_rows, dv = GQA_HEADS, GQA_HEADS // GQA_KV_HEADS, GQA_HEAD_DIM, GQA_HEAD_DIM
        else:
            win_t = mla_w_in[j].T.astype(BF16)
            wuq = mla_w_uq[j].reshape(MLA_Q_RANK, MLA_HEADS, MLA_QK_DIM)
            wuq = jnp.pad(wuq, ((0, 0), (0, 0), (0, MLA_QK_PAD - MLA_QK_DIM)))
            wuq_t = wuq.reshape(MLA_Q_RANK, MLA_HEADS * MLA_QK_PAD).T.astype(BF16)
            wukv = mla_w_ukv[j].reshape(MLA_KV_RANK, MLA_HEADS, MLA_NOPE_DIM + MLA_V_DIM)
            wukv = jnp.concatenate([wukv[:, :, :MLA_NOPE_DIM].reshape(MLA_KV_RANK, -1),
                                    wukv[:, :, MLA_NOPE_DIM:].reshape(MLA_KV_RANK, -1)], axis=1)
            wukv_t = wukv.T.astype(BF16)
            wo_t = mla_w_o[j].T.astype(BF16)
            gq, gkv = _col(mla_q_norm[j]), _col(mla_kv_norm[j])
            q_l, k_l, v_l = _mla_proj(x_t, modc, i, lat_mod, win_t, gq, gkv, wuq_t, wukv_t, rope_mla, LAT_TILE)
            q_c, k_c, v_c = _mla_proj(xc_t, modc, i, ctx_mod, win_t, gq, gkv, wuq_t, wukv_t, None, n_ctx)
            heads, group, dk_rows, dv = MLA_HEADS, 1, MLA_QK_PAD, MLA_V_DIM

        a_l = _attention(q_l, [(k_c, v_c), (k_l, v_l)], heads, group, dk_rows, dv, LAT_TILE, "attn_lat")
        x_t = _post(x_t, a_l, modc, ln_cols, i, lat_mod, wo_t, w1_t, w2_t, LAT_TILE, "post_lat")
        if need_ctx:
            a_c = _attention(q_c, [(k_c, v_c)], heads, group, dk_rows, dv, n_ctx, "attn_ctx")
            xc_t = _post(xc_t, a_c, modc, ln_cols, i, ctx_mod, wo_t, w1_t, w2_t, n_ctx, "post_ctx")

    return jnp.swapaxes(x_t, 1, 2)
```

```python
import functools

import jax
import jax.numpy as jnp
from jax import lax
from jax.experimental import pallas as pl
from jax.experimental.pallas import tpu as pltpu

D_MODEL = 1024
DEPTH = 4
GRID_W = 64
N_MIXERS = 2
GQA_HEADS = 16
GQA_KV_HEADS = 4
GQA_HEAD_DIM = 64
MLA_HEADS = 16
MLA_Q_RANK = 384
MLA_KV_RANK = 256
MLA_NOPE_DIM = 64
MLA_ROPE_DIM = 32
MLA_V_DIM = 64
MLA_QK_DIM = MLA_NOPE_DIM + MLA_ROPE_DIM
MLA_QK_PAD = 128
FFN_HIDDEN = 4 * D_MODEL
ROPE_THETA = 10000.0
NORM_EPS = 1e-6
DEEPNORM_ALPHA = (2.0 * DEPTH) ** 0.25
LOG2E = 1.4426950408889634

HEAD_V_DIM = 64
ONES_ROWS = 16
V_ROWS = HEAD_V_DIM + ONES_ROWS

LAT_TILE = 512
KV_CHUNK = 256
FFN_CHUNK = 1024
WEIGHT_TILE = 1024
MOD_ROWS = 16
VMEM_LIMIT_BYTES = 56 * 1024 * 1024

F32 = jnp.float32
BF16 = jnp.bfloat16


def _params(n_axes):
    return pltpu.CompilerParams(dimension_semantics=("arbitrary",) * n_axes,
                                vmem_limit_bytes=VMEM_LIMIT_BYTES)


def _transpose_cast_kernel(w_ref, o_ref):
    o_ref[0] = w_ref[0].T.astype(BF16)


def _transpose_cast(w):
    n_l, k, m = w.shape
    tk = WEIGHT_TILE if k % WEIGHT_TILE == 0 else WEIGHT_TILE // 2
    tm = WEIGHT_TILE if m % WEIGHT_TILE == 0 else WEIGHT_TILE // 2
    return pl.pallas_call(
        _transpose_cast_kernel,
        grid=(n_l, k // tk, m // tm),
        in_specs=[pl.BlockSpec((1, tk, tm), lambda l, i, j: (l, i, j))],
        out_specs=pl.BlockSpec((1, tm, tk), lambda l, i, j: (l, j, i)),
        out_shape=jax.ShapeDtypeStruct((n_l, m, k), BF16),
        compiler_params=_params(3),
        name="weight_transpose_cast",
    )(w)


def _layer_spec(w, idx, **kw):
    return pl.BlockSpec((1,) + w.shape[1:], lambda bi, ti: (idx, 0, 0), **kw)


def _mod_kernel(cv_ref, w_ref, b_ref, o_ref):
    c = cv_ref[...]
    s = c * jax.nn.sigmoid(c)
    o_ref[0] = jnp.dot(s, w_ref[0], precision=lax.Precision.HIGHEST,
                       preferred_element_type=F32) + b_ref[0]


def _modulation(cv, w_ada, b_ada):
    d = D_MODEL
    return pl.pallas_call(
        _mod_kernel,
        grid=(DEPTH, 6),
        in_specs=[pl.BlockSpec((MOD_ROWS, d), lambda i, j: (0, 0)),
                  pl.BlockSpec((1, d, d), lambda i, j: (i, 0, j)),
                  pl.BlockSpec((1, 1, d), lambda i, j: (i, 0, j))],
        out_specs=pl.BlockSpec((1, MOD_ROWS, d), lambda i, j: (i, 0, j)),
        out_shape=jax.ShapeDtypeStruct((DEPTH, MOD_ROWS, 6 * d), F32),
        compiler_params=_params(2),
        name="adaln_modulation",
    )(cv, w_ada, b_ada.reshape(DEPTH, 1, 6 * d))


def _modulate(x, mod_ref):
    d = D_MODEL
    shift = mod_ref[0, 0, 0:d, :]
    scale = mod_ref[0, 0, d:2 * d, :]
    return x * (1.0 + scale) + shift


def _rms_rows(x, gain_col):
    ms = jnp.mean(x * x, axis=0, keepdims=True)
    return x * lax.rsqrt(ms + NORM_EPS) * gain_col


def _layer_norm_rows(z, g_col, b_col):
    mu = jnp.mean(z, axis=0, keepdims=True)
    zc = z - mu
    var = jnp.mean(zc * zc, axis=0, keepdims=True)
    return zc * lax.rsqrt(var + NORM_EPS) * g_col + b_col


def _rotate_rows(x, rope_ref):
    q = x.shape[0] // 4
    cr, sr, cc, sc = rope_ref[0], rope_ref[1], rope_ref[2], rope_ref[3]
    a1, a2, b1, b2 = x[0:q], x[q:2 * q], x[2 * q:3 * q], x[3 * q:4 * q]
    return jnp.concatenate([a1 * cr - a2 * sr, a2 * cr + a1 * sr,
                            b1 * cc - b2 * sc, b2 * cc + b1 * sc], axis=0)


def _store_values(v_ref, v, n_heads):
    ones = jnp.ones((ONES_ROWS, v.shape[1]), BF16)
    for j in range(n_heads):
        v_ref[0, j * V_ROWS:j * V_ROWS + HEAD_V_DIM, :] = v[j * HEAD_V_DIM:(j + 1) * HEAD_V_DIM].astype(BF16)
        v_ref[0, j * V_ROWS + HEAD_V_DIM:(j + 1) * V_ROWS, :] = ones


def _gqa_proj_kernel(x_ref, mod_ref, w_ref, gq_ref, gk_ref, *rest, rotate, natural_in):
    rest = list(rest)
    rope_ref = rest.pop(0) if rotate else None
    q_ref, k_ref, v_ref = rest[:3]
    hd = GQA_HEAD_DIM
    nq = GQA_HEADS * hd
    nk = GQA_KV_HEADS * hd
    if natural_in:
        x = x_ref[0].T
        rest[3][0] = x
    else:
        x = x_ref[0]
    h = _modulate(x, mod_ref).astype(BF16)
    p = jnp.dot(w_ref[0], h, preferred_element_type=F32)
    gq = gq_ref[...]
    gk = gk_ref[...]
    q_scale = hd ** -0.5 * LOG2E
    for i in range(GQA_HEADS):
        qh = _rms_rows(p[i * hd:(i + 1) * hd], gq)
        if rotate:
            qh = _rotate_rows(qh, rope_ref)
        q_ref[0, i * hd:(i + 1) * hd, :] = (qh * q_scale).astype(BF16)
    ks = []
    for j in range(GQA_KV_HEADS):
        kh = _rms_rows(p[nq + j * hd:nq + (j + 1) * hd], gk)
        if rotate:
            kh = _rotate_rows(kh, rope_ref)
        ks.append(kh)
    kt = jnp.concatenate(ks, axis=0).T
    for j in range(GQA_KV_HEADS):
        k_ref[0, j] = kt[:, j * hd:(j + 1) * hd].astype(BF16)
    _store_values(v_ref, p[nq + nk:nq + 2 * nk], GQA_KV_HEADS)


def _gqa_proj(x, natural_in, modc, layer, mod_index, w_t, w_idx, gq_col, gk_col, rope, tile):
    if natural_in:
        b, n, d = x.shape
        x_spec = pl.BlockSpec((1, tile, d), lambda bi, ti: (bi, ti, 0))
    else:
        b, d, n = x.shape
        x_spec = pl.BlockSpec((1, d, tile), lambda bi, ti: (bi, 0, ti))
    hd = GQA_HEAD_DIM
    nq = GQA_HEADS * hd
    rotate = rope is not None
    in_specs = [x_spec,
                pl.BlockSpec((1, 1, 2 * d, 1), lambda bi, ti: (layer, mod_index(bi), 0, 0)),
                _layer_spec(w_t, w_idx),
                pl.BlockSpec(gq_col.shape, lambda bi, ti: (0, 0)),
                pl.BlockSpec(gk_col.shape, lambda bi, ti: (0, 0))]
    args = [x, modc, w_t, gq_col, gk_col]
    if rotate:
        in_specs.append(pl.BlockSpec((4, hd // 4, tile), lambda bi, ti: (0, 0, ti)))
        args.append(rope)
    out_specs = [pl.BlockSpec((1, nq, tile), lambda bi, ti: (bi, 0, ti)),
                 pl.BlockSpec((1, GQA_KV_HEADS, tile, hd), lambda bi, ti: (bi, 0, ti, 0)),
                 pl.BlockSpec((1, GQA_KV_HEADS * V_ROWS, tile), lambda bi, ti: (bi, 0, ti))]
    out_shape = [jax.ShapeDtypeStruct((b, nq, n), BF16),
                 jax.ShapeDtypeStruct((b, GQA_KV_HEADS, n, hd), BF16),
                 jax.ShapeDtypeStruct((b, GQA_KV_HEADS * V_ROWS, n), BF16)]
    if natural_in:
        out_specs.append(pl.BlockSpec((1, d, tile), lambda bi, ti: (bi, 0, ti)))
        out_shape.append(jax.ShapeDtypeStruct((b, d, n), F32))
    return pl.pallas_call(
        functools.partial(_gqa_proj_kernel, rotate=rotate, natural_in=natural_in),
        grid=(b, n // tile),
        in_specs=in_specs,
        out_specs=out_specs,
        out_shape=out_shape,
        compiler_params=_params(2),
        name="gqa_proj_lat" if rotate else "gqa_proj_ctx",
    )(*args)


def _mla_proj_kernel(x_ref, mod_ref, win_ref, gq_ref, gkv_ref, wuq_ref, wukv_ref, *rest, rotate):
    if rotate:
        rope_ref, q_ref, k_ref, v_ref = rest
    else:
        q_ref, k_ref, v_ref = rest
    nh, nope, rd, pad = MLA_HEADS, MLA_NOPE_DIM, MLA_ROPE_DIM, MLA_QK_PAD
    t = x_ref.shape[2]
    h = _modulate(x_ref[0], mod_ref).astype(BF16)
    p = jnp.dot(win_ref[0], h, preferred_element_type=F32)
    cq = _rms_rows(p[0:MLA_Q_RANK], gq_ref[...]).astype(BF16)
    ckv = _rms_rows(p[MLA_Q_RANK:MLA_Q_RANK + MLA_KV_RANK], gkv_ref[...]).astype(BF16)
    k_pe = p[MLA_Q_RANK + MLA_KV_RANK:MLA_Q_RANK + MLA_KV_RANK + rd]
    if rotate:
        k_pe = _rotate_rows(k_pe, rope_ref)
    q = jnp.dot(wuq_ref[0], cq, preferred_element_type=F32)
    kv = jnp.dot(wukv_ref[0], ckv, preferred_element_type=F32)
    q_scale = MLA_QK_DIM ** -0.5 * LOG2E
    zeros = jnp.zeros((pad - nope - rd, t), F32)
    for i in range(nh):
        qh = q[i * pad:(i + 1) * pad]
        q_pe = qh[nope:nope + rd]
        if rotate:
            q_pe = _rotate_rows(q_pe, rope_ref)
        qh = jnp.concatenate([qh[0:nope], q_pe, qh[nope + rd:pad]], axis=0)
        q_ref[0, i * pad:(i + 1) * pad, :] = (qh * q_scale).astype(BF16)
        kh = jnp.concatenate([kv[i * nope:(i + 1) * nope], k_pe, zeros], axis=0)
        k_ref[0, i] = kh.T.astype(BF16)
    _store_values(v_ref, kv[nh * nope:], nh)


def _mla_proj(x_t, modc, layer, mod_index, w_idx, win_t, gq_col, gkv_col, wuq_t, wukv_t, rope, tile):
    b, d, n = x_t.shape
    nh, pad = MLA_HEADS, MLA_QK_PAD
    rotate = rope is not None
    in_specs = [pl.BlockSpec((1, d, tile), lambda bi, ti: (bi, 0, ti)),
                pl.BlockSpec((1, 1, 2 * d, 1), lambda bi, ti: (layer, mod_index(bi), 0, 0)),
                _layer_spec(win_t, w_idx),
                pl.BlockSpec(gq_col.shape, lambda bi, ti: (0, 0)),
                pl.BlockSpec(gkv_col.shape, lambda bi, ti: (0, 0)),
                _layer_spec(wuq_t, w_idx),
                _layer_spec(wukv_t, w_idx)]
    args = [x_t, modc, win_t, gq_col, gkv_col, wuq_t, wukv_t]
    if rotate:
        in_specs.append(pl.BlockSpec((4, MLA_ROPE_DIM // 4, tile), lambda bi, ti: (0, 0, ti)))
        args.append(rope)
    return pl.pallas_call(
        functools.partial(_mla_proj_kernel, rotate=rotate),
        grid=(b, n // tile),
        in_specs=in_specs,
        out_specs=[pl.BlockSpec((1, nh * pad, tile), lambda bi, ti: (bi, 0, ti)),
                   pl.BlockSpec((1, nh, tile, pad), lambda bi, ti: (bi, 0, ti, 0)),
                   pl.BlockSpec((1, nh * V_ROWS, tile), lambda bi, ti: (bi, 0, ti))],
        out_shape=[jax.ShapeDtypeStruct((b, nh * pad, n), BF16),
                   jax.ShapeDtypeStruct((b, nh, n, pad), BF16),
                   jax.ShapeDtypeStruct((b, nh * V_ROWS, n), BF16)],
        compiler_params=_params(2),
        name="mla_proj_lat" if rotate else "mla_proj_ctx",
    )(*args)


def _attn_kernel(q_ref, *rest, n_heads, group, dk_rows, dv):
    n_src = (len(rest) - 3) // 2
    k_refs = rest[0:2 * n_src:2]
    v_refs = rest[1:2 * n_src:2]
    o_ref = rest[2 * n_src]
    s_refs = rest[2 * n_src + 1:]
    chunks = []
    base = 0
    for src, k_ref in enumerate(k_refs):
        for lo in range(0, k_ref.shape[2], KV_CHUNK):
            chunks.append((src, lo, base + lo))
        base += k_ref.shape[2]

    def step(cur, nxt):
        if nxt is not None:
            h_n, slot_n = nxt
            kvh_n = h_n // group
            q = q_ref[0, pl.ds(pl.multiple_of(h_n * dk_rows, dk_rows), dk_rows), :]
        if cur is not None:
            h_c, slot_c, m_c = cur
            v_row = pl.multiple_of((h_c // group) * V_ROWS, ONES_ROWS)
        m_n = None
        acc = None
        for src, lo, glo in chunks:
            if nxt is not None:
                s = jnp.dot(k_refs[src][0, kvh_n, lo:lo + KV_CHUNK, :], q,
                            preferred_element_type=F32)
                s_refs[slot_n][glo:glo + KV_CHUNK, :] = s
                ms = jnp.max(s, axis=0, keepdims=True)
                m_n = ms if m_n is None else jnp.maximum(m_n, ms)
            if cur is not None:
                p = jnp.exp2(s_refs[slot_c][glo:glo + KV_CHUNK, :] - m_c).astype(BF16)
                part = jnp.dot(v_refs[src][0, pl.ds(v_row, V_ROWS), lo:lo + KV_CHUNK], p,
                               preferred_element_type=F32)
                acc = part if acc is None else acc + part
        if cur is not None:
            out = acc[0:dv] / acc[dv:dv + 1]
            o_ref[0, pl.ds(pl.multiple_of(h_c * dv, dv), dv), :] = out.astype(BF16)
        return m_n

    def pair(i, m_even):
        h0 = 2 * i
        m_odd = step((h0, 0, m_even), (h0 + 1, 1))
        return step((h0 + 1, 1, m_odd), (h0 + 2, 0))

    m_even = lax.fori_loop(0, n_heads // 2 - 1, pair, step(None, (0, 0)))
    m_odd = step((n_heads - 2, 0, m_even), (n_heads - 1, 1))
    step((n_heads - 1, 1, m_odd), None)


def _attention(q_t, kv_sources, batch, n_heads, group, dk_rows, dv, tq, name):
    qb, qrows, nq = q_t.shape
    if qb == 1:
        q_index = lambda bi, ti: (0, 0, bi)
        grid = (batch, 1)
    else:
        q_index = lambda bi, ti: (bi, 0, ti)
        grid = (batch, nq // tq)
    in_specs = [pl.BlockSpec((1, qrows, tq), q_index)]
    args = [q_t]
    n_keys = 0
    for k, v, keys in kv_sources:
        if k.shape[0] == 1:
            k_index, v_index = (lambda bi, ti: (0, 0, bi, 0)), (lambda bi, ti: (0, 0, bi))
        else:
            k_index, v_index = (lambda bi, ti: (bi, 0, 0, 0)), (lambda bi, ti: (bi, 0, 0))
        in_specs.append(pl.BlockSpec((1, k.shape[1], keys, k.shape[3]), k_index))
        in_specs.append(pl.BlockSpec((1, v.shape[1], keys), v_index))
        args += [k, v]
        n_keys += keys
    return pl.pallas_call(
        functools.partial(_attn_kernel, n_heads=n_heads, group=group, dk_rows=dk_rows, dv=dv),
        grid=grid,
        in_specs=in_specs,
        out_specs=pl.BlockSpec((1, n_heads * dv, tq), q_index),
        out_shape=jax.ShapeDtypeStruct((qb, n_heads * dv, nq), BF16),
        scratch_shapes=[pltpu.VMEM((n_keys, tq), F32), pltpu.VMEM((n_keys, tq), F32)],
        compiler_params=_params(2),
        name=name,
    )(*args)


def _post_kernel(x_ref, a_ref, moda_ref, modm_ref, ln_ref, wo_ref, w1_ref, w2_ref, o_ref, *, natural_out):
    d = D_MODEL
    x = x_ref[0]
    gate_a = moda_ref[0, 0, 0:d, :]
    shift_m = moda_ref[0, 0, d:2 * d, :]
    scale_m = modm_ref[0, 0, 0:d, :]
    gate_m = modm_ref[0, 0, d:2 * d, :]
    y = jnp.dot(wo_ref[0], a_ref[0], preferred_element_type=F32)
    x1 = _layer_norm_rows(DEEPNORM_ALPHA * x + gate_a * y, ln_ref[0, 0], ln_ref[0, 1])
    h = (x1 * (1.0 + scale_m) + shift_m).astype(BF16)
    acc = None
    for c in range(FFN_HIDDEN // FFN_CHUNK):
        u = jnp.dot(w1_ref[0, c * FFN_CHUNK:(c + 1) * FFN_CHUNK, :], h, preferred_element_type=F32)
        u = jnp.maximum(u, 0.0)
        u = (u * u).astype(BF16)
        part = jnp.dot(w2_ref[0, :, c * FFN_CHUNK:(c + 1) * FFN_CHUNK], u, preferred_element_type=F32)
        acc = part if acc is None else acc + part
    x2 = _layer_norm_rows(DEEPNORM_ALPHA * x1 + gate_m * acc, ln_ref[0, 2], ln_ref[0, 3])
    o_ref[0] = x2.T if natural_out else x2


def _post(x_t, attn_t, modc, ln_cols, layer, mod_index, wo_t, wo_idx, w1_t, w2_t, tile, natural_out, name):
    b, d, n = x_t.shape
    single = dict(pipeline_mode=pl.Buffered(1))
    if natural_out:
        out_spec = pl.BlockSpec((1, tile, d), lambda bi, ti: (bi, ti, 0))
        out_shape = jax.ShapeDtypeStruct((b, n, d), F32)
    else:
        out_spec = pl.BlockSpec((1, d, tile), lambda bi, ti: (bi, 0, ti))
        out_shape = jax.ShapeDtypeStruct((b, d, n), F32)
    return pl.pallas_call(
        functools.partial(_post_kernel, natural_out=natural_out),
        grid=(b, n // tile),
        in_specs=[pl.BlockSpec((1, d, tile), lambda bi, ti: (bi, 0, ti)),
                  pl.BlockSpec((1, d, tile), lambda bi, ti: (bi, 0, ti)),
                  pl.BlockSpec((1, 1, 2 * d, 1), lambda bi, ti: (layer, mod_index(bi), 1, 0)),
                  pl.BlockSpec((1, 1, 2 * d, 1), lambda bi, ti: (layer, mod_index(bi), 2, 0)),
                  pl.BlockSpec((1, 4, d, 1), lambda bi, ti: (layer, 0, 0, 0)),
                  _layer_spec(wo_t, wo_idx, **single),
                  _layer_spec(w1_t, layer, **single),
                  _layer_spec(w2_t, layer, **single)],
        out_specs=out_spec,
        out_shape=out_shape,
        compiler_params=_params(2),
        name=name,
    )(x_t, attn_t, modc, modc, ln_cols, wo_t, w1_t, w2_t)


def _rope_tables_t(n_tok, rot_dim):
    rows = n_tok // GRID_W
    row = jnp.broadcast_to(jnp.arange(rows, dtype=F32)[:, None], (rows, GRID_W)).reshape(-1)
    col = jnp.broadcast_to(jnp.arange(GRID_W, dtype=F32)[None, :], (rows, GRID_W)).reshape(-1)
    axis_dim = rot_dim // 2
    inv_freq = ROPE_THETA ** (-jnp.arange(0, axis_dim, 2, dtype=F32) / axis_dim)
    ang_row = inv_freq[:, None] * row[None, :]
    ang_col = inv_freq[:, None] * col[None, :]
    return jnp.stack([jnp.cos(ang_row), jnp.sin(ang_row), jnp.cos(ang_col), jnp.sin(ang_col)])


def _col(v):
    return v.astype(F32)[:, None]


def kernel(x, c, ctx, c_ctx, w_ada, b_ada, ln_g, ln_b, mlp_w1, mlp_w2, gqa_w_qkv, gqa_q_norm, gqa_k_norm, gqa_w_o, mla_w_in, mla_q_norm, mla_kv_norm, mla_w_uq, mla_w_ukv, mla_w_o):
    b, n_lat, d = x.shape
    n_ctx = ctx.shape[1]

    cv = jnp.concatenate([c, c_ctx[None, :], jnp.zeros((MOD_ROWS - b - 1, d), F32)], axis=0)
    mod = _modulation(cv, w_ada, b_ada)
    modc = mod[:, :b + 1, :, None]
    ln_cols = jnp.stack([ln_g[:, 0], ln_b[:, 0], ln_g[:, 1], ln_b[:, 1]], axis=1)[..., None]

    lat_mod = lambda bi: bi
    ctx_mod = lambda bi: b

    rope_gqa = _rope_tables_t(n_lat, GQA_HEAD_DIM)
    rope_mla = _rope_tables_t(n_lat, MLA_ROPE_DIM)

    w1_t, w2_t = _transpose_cast(mlp_w1), _transpose_cast(mlp_w2)
    gqa_w_t, gqa_wo_t = _transpose_cast(gqa_w_qkv), _transpose_cast(gqa_w_o)
    mla_wo_t = _transpose_cast(mla_w_o)
    n_b = mla_w_in.shape[0]
    mla_win_t = mla_w_in.swapaxes(1, 2).astype(BF16)
    wuq = mla_w_uq.reshape(n_b, MLA_Q_RANK, MLA_HEADS, MLA_QK_DIM)
    wuq = jnp.pad(wuq, ((0, 0), (0, 0), (0, 0), (0, MLA_QK_PAD - MLA_QK_DIM)))
    mla_wuq_t = wuq.reshape(n_b, MLA_Q_RANK, MLA_HEADS * MLA_QK_PAD).swapaxes(1, 2).astype(BF16)
    wukv = mla_w_ukv.reshape(n_b, MLA_KV_RANK, MLA_HEADS, MLA_NOPE_DIM + MLA_V_DIM)
    wukv = jnp.concatenate([wukv[..., :MLA_NOPE_DIM].reshape(n_b, MLA_KV_RANK, -1),
                            wukv[..., MLA_NOPE_DIM:].reshape(n_b, MLA_KV_RANK, -1)], axis=2)
    mla_wukv_t = wukv.swapaxes(1, 2).astype(BF16)

    xc_nat = ctx.reshape(1, b * n_ctx, d)
    x_t = xc_t = None

    for i in range(DEPTH):
        need_ctx = i < DEPTH - 1
        last = i == DEPTH - 1
        j = i // N_MIXERS
        if i % N_MIXERS == 0:
            wo_t = gqa_wo_t
            gq, gk = _col(gqa_q_norm[j]), _col(gqa_k_norm[j])
            if i == 0:
                q_l, k_l, v_l, x_t = _gqa_proj(x, True, modc, i, lat_mod, gqa_w_t, j, gq, gk, rope_gqa, LAT_TILE)
                q_c, k_c, v_c, xc_t = _gqa_proj(xc_nat, True, modc, i, ctx_mod, gqa_w_t, j, gq, gk, None, LAT_TILE)
            else:
                q_l, k_l, v_l = _gqa_proj(x_t, False, modc, i, lat_mod, gqa_w_t, j, gq, gk, rope_gqa, LAT_TILE)
                q_c, k_c, v_c = _gqa_proj(xc_t, False, modc, i, ctx_mod, gqa_w_t, j, gq, gk, None, LAT_TILE)
            heads, group, dk_rows, dv = GQA_HEADS, GQA_HEADS // GQA_KV_HEADS, GQA_HEAD_DIM, GQA_HEAD_DIM
        else:
            wo_t = mla_wo_t
            gq, gkv = _col(mla_q_norm[j]), _col(mla_kv_norm[j])
            q_l, k_l, v_l = _mla_proj(x_t, modc, i, lat_mod, j, mla_win_t, gq, gkv, mla_wuq_t, mla_wukv_t,
                                      rope_mla, LAT_TILE)
            q_c, k_c, v_c = _mla_proj(xc_t, modc, i, ctx_mod, j, mla_win_t, gq, gkv, mla_wuq_t, mla_wukv_t,
                                      None, LAT_TILE)
            heads, group, dk_rows, dv = MLA_HEADS, 1, MLA_QK_PAD, MLA_V_DIM

        a_l = _attention(q_l, [(k_c, v_c, n_ctx), (k_l, v_l, n_lat)], b, heads, group, dk_rows, dv, LAT_TILE,
                         "attn_lat")
        x_t = _post(x_t, a_l, modc, ln_cols, i, lat_mod, wo_t, j, w1_t, w2_t, LAT_TILE, last, "post_lat")
        if need_ctx:
            a_c = _attention(q_c, [(k_c, v_c, n_ctx)], b, heads, group, dk_rows, dv, n_ctx, "attn_ctx")
            xc_t = _post(xc_t, a_c, modc, ln_cols, i, ctx_mod, wo_t, j, w1_t, w2_t, LAT_TILE, False, "post_ctx")

    return x_t
```

```python
import functools

import jax
import jax.numpy as jnp
from jax import lax
from jax.experimental import pallas as pl
from jax.experimental.pallas import tpu as pltpu

D_MODEL = 1024
DEPTH = 4
GRID_W = 64
N_MIXERS = 2
GQA_HEADS = 16
GQA_KV_HEADS = 4
GQA_HEAD_DIM = 64
MLA_HEADS = 16
MLA_Q_RANK = 384
MLA_KV_RANK = 256
MLA_NOPE_DIM = 64
MLA_ROPE_DIM = 32
MLA_V_DIM = 64
MLA_QK_DIM = MLA_NOPE_DIM + MLA_ROPE_DIM
MLA_QK_PAD = 128
FFN_HIDDEN = 4 * D_MODEL
ROPE_THETA = 10000.0
NORM_EPS = 1e-6
DEEPNORM_ALPHA = (2.0 * DEPTH) ** 0.25
LOG2E = 1.4426950408889634

HEAD_V_DIM = 64
ONES_ROWS = 16
V_ROWS = HEAD_V_DIM + ONES_ROWS

LAT_TILE = 512
POST_TILE = 1024
KV_CHUNK = 256
FFN_CHUNK = 1024
WEIGHT_TILE = 1024
MOD_ROWS = 16
PROJ_ROWS = 256
LANES = 128
VMEM_LIMIT_BYTES = 60 * 1024 * 1024

F32 = jnp.float32
BF16 = jnp.bfloat16


def _params(n_axes):
    return pltpu.CompilerParams(dimension_semantics=("arbitrary",) * n_axes,
                                vmem_limit_bytes=VMEM_LIMIT_BYTES)


def _transpose_cast_kernel(w_ref, o_ref):
    o_ref[0] = w_ref[0].T.astype(BF16)


def _transpose_cast(w):
    n_l, k, m = w.shape
    tk = WEIGHT_TILE if k % WEIGHT_TILE == 0 else WEIGHT_TILE // 2
    tm = WEIGHT_TILE if m % WEIGHT_TILE == 0 else WEIGHT_TILE // 2
    return pl.pallas_call(
        _transpose_cast_kernel,
        grid=(n_l, k // tk, m // tm),
        in_specs=[pl.BlockSpec((1, tk, tm), lambda l, i, j: (l, i, j))],
        out_specs=pl.BlockSpec((1, tm, tk), lambda l, i, j: (l, j, i)),
        out_shape=jax.ShapeDtypeStruct((n_l, m, k), BF16),
        compiler_params=_params(3),
        name="weight_transpose_cast",
    )(w)


def _layer_spec(w, idx, **kw):
    return pl.BlockSpec((1,) + w.shape[1:], lambda bi, ti: (idx, 0, 0), **kw)


def _mod_kernel(cv_ref, w_ref, b_ref, o_ref):
    c = cv_ref[...]
    s = c * jax.nn.sigmoid(c)
    w = w_ref[0]
    s_hi = s.astype(BF16)
    s_lo = (s - s_hi.astype(F32)).astype(BF16)
    w_hi = w.astype(BF16)
    w_lo = (w - w_hi.astype(F32)).astype(BF16)
    dot = functools.partial(jnp.dot, preferred_element_type=F32)
    o_ref[0] = dot(s_hi, w_hi) + (dot(s_lo, w_hi) + dot(s_hi, w_lo)) + b_ref[0]


def _modulation(cv, w_ada, b_ada):
    d = D_MODEL
    return pl.pallas_call(
        _mod_kernel,
        grid=(DEPTH, 6),
        in_specs=[pl.BlockSpec((MOD_ROWS, d), lambda i, j: (0, 0)),
                  pl.BlockSpec((1, d, d), lambda i, j: (i, 0, j)),
                  pl.BlockSpec((1, 1, d), lambda i, j: (i, 0, j))],
        out_specs=pl.BlockSpec((1, MOD_ROWS, d), lambda i, j: (i, 0, j)),
        out_shape=jax.ShapeDtypeStruct((DEPTH, MOD_ROWS, 6 * d), F32),
        compiler_params=_params(2),
        name="adaln_modulation",
    )(cv, w_ada, b_ada.reshape(DEPTH, 1, 6 * d))


def _modulate(x, mod_ref):
    d = D_MODEL
    shift = mod_ref[0, 0, 0:d, :]
    scale = mod_ref[0, 0, d:2 * d, :]
    return x * (1.0 + scale) + shift


def _rms_rows(x, gain_col):
    ms = jnp.mean(x * x, axis=0, keepdims=True)
    return x * lax.rsqrt(ms + NORM_EPS) * gain_col


def _layer_norm_rows(z, g_col, b_col):
    mu = jnp.mean(z, axis=0, keepdims=True)
    zc = z - mu
    var = jnp.mean(zc * zc, axis=0, keepdims=True)
    return zc * lax.rsqrt(var + NORM_EPS) * g_col + b_col


def _rotate_rows(x, rope_ref):
    q = x.shape[0] // 4
    cr, sr, cc, sc = rope_ref[0], rope_ref[1], rope_ref[2], rope_ref[3]
    a1, a2, b1, b2 = x[0:q], x[q:2 * q], x[2 * q:3 * q], x[3 * q:4 * q]
    return jnp.concatenate([a1 * cr - a2 * sr, a2 * cr + a1 * sr,
                            b1 * cc - b2 * sc, b2 * cc + b1 * sc], axis=0)


def _store_values(v_ref, v, n_heads):
    ones = jnp.ones((ONES_ROWS, v.shape[1]), BF16)
    for j in range(n_heads):
        v_ref[0, j * V_ROWS:j * V_ROWS + HEAD_V_DIM, :] = v[j * HEAD_V_DIM:(j + 1) * HEAD_V_DIM].astype(BF16)
        v_ref[0, j * V_ROWS + HEAD_V_DIM:(j + 1) * V_ROWS, :] = ones


def _gqa_proj_kernel(x_ref, mod_ref, w_ref, gq_ref, gk_ref, *rest, rotate, natural_in):
    rest = list(rest)
    rope_ref = rest.pop(0) if rotate else None
    q_ref, k_ref, v_ref = rest[:3]
    hd = GQA_HEAD_DIM
    nq = GQA_HEADS * hd
    nk = GQA_KV_HEADS * hd
    if natural_in:
        x = x_ref[0].T
        rest[3][0] = x
    else:
        x = x_ref[0]
    h = _modulate(x, mod_ref).astype(BF16)
    gq = gq_ref[...]
    gk = gk_ref[...]
    q_scale = hd ** -0.5 * LOG2E
    heads_per_step = PROJ_ROWS // hd

    def project(row0, rows):
        return jnp.dot(w_ref[0, row0:row0 + rows, :], h, preferred_element_type=F32)

    for g in range(nq // PROJ_ROWS):
        p = project(g * PROJ_ROWS, PROJ_ROWS)
        for i in range(heads_per_step):
            qh = _rms_rows(p[i * hd:(i + 1) * hd], gq)
            if rotate:
                qh = _rotate_rows(qh, rope_ref)
            row = g * PROJ_ROWS + i * hd
            q_ref[0, row:row + hd, :] = (qh * q_scale).astype(BF16)
    p = project(nq, nk)
    ks = []
    for j in range(GQA_KV_HEADS):
        kh = _rms_rows(p[j * hd:(j + 1) * hd], gk)
        if rotate:
            kh = _rotate_rows(kh, rope_ref)
        ks.append(kh)
    kt = jnp.concatenate(ks, axis=0).T
    for j in range(GQA_KV_HEADS):
        k_ref[0, j] = kt[:, j * hd:(j + 1) * hd].astype(BF16)
    _store_values(v_ref, project(nq + nk, nk), GQA_KV_HEADS)


def _gqa_proj(x, natural_in, modc, layer, mod_index, w_t, w_idx, gq_col, gk_col, rope, tile):
    if natural_in:
        b, n, d = x.shape
        x_spec = pl.BlockSpec((1, tile, d), lambda bi, ti: (bi, ti, 0))
    else:
        b, d, n = x.shape
        x_spec = pl.BlockSpec((1, d, tile), lambda bi, ti: (bi, 0, ti))
    hd = GQA_HEAD_DIM
    nq = GQA_HEADS * hd
    rotate = rope is not None
    in_specs = [x_spec,
                pl.BlockSpec((1, 1, 2 * d, 1), lambda bi, ti: (layer, mod_index(bi), 0, 0)),
                _layer_spec(w_t, w_idx),
                pl.BlockSpec(gq_col.shape, lambda bi, ti: (0, 0)),
                pl.BlockSpec(gk_col.shape, lambda bi, ti: (0, 0))]
    args = [x, modc, w_t, gq_col, gk_col]
    if rotate:
        in_specs.append(pl.BlockSpec((4, hd // 4, tile), lambda bi, ti: (0, 0, ti)))
        args.append(rope)
    out_specs = [pl.BlockSpec((1, nq, tile), lambda bi, ti: (bi, 0, ti)),
                 pl.BlockSpec((1, GQA_KV_HEADS, tile, hd), lambda bi, ti: (bi, 0, ti, 0)),
                 pl.BlockSpec((1, GQA_KV_HEADS * V_ROWS, tile), lambda bi, ti: (bi, 0, ti))]
    out_shape = [jax.ShapeDtypeStruct((b, nq, n), BF16),
                 jax.ShapeDtypeStruct((b, GQA_KV_HEADS, n, hd), BF16),
                 jax.ShapeDtypeStruct((b, GQA_KV_HEADS * V_ROWS, n), BF16)]
    if natural_in:
        out_specs.append(pl.BlockSpec((1, d, tile), lambda bi, ti: (bi, 0, ti)))
        out_shape.append(jax.ShapeDtypeStruct((b, d, n), F32))
    return pl.pallas_call(
        functools.partial(_gqa_proj_kernel, rotate=rotate, natural_in=natural_in),
        grid=(b, n // tile),
        in_specs=in_specs,
        out_specs=out_specs,
        out_shape=out_shape,
        compiler_params=_params(2),
        name="gqa_proj_lat" if rotate else "gqa_proj_ctx",
    )(*args)


def _mla_proj_kernel(x_ref, mod_ref, win_ref, gq_ref, gkv_ref, wuq_ref, wukv_ref, *rest, rotate):
    if rotate:
        rope_ref, q_ref, k_ref, v_ref = rest
    else:
        q_ref, k_ref, v_ref = rest
    nh, nope, rd, pad = MLA_HEADS, MLA_NOPE_DIM, MLA_ROPE_DIM, MLA_QK_PAD
    t = x_ref.shape[2]
    h = _modulate(x_ref[0], mod_ref).astype(BF16)
    p = jnp.dot(win_ref[0], h, preferred_element_type=F32)
    cq = _rms_rows(p[0:MLA_Q_RANK], gq_ref[...]).astype(BF16)
    ckv = _rms_rows(p[MLA_Q_RANK:MLA_Q_RANK + MLA_KV_RANK], gkv_ref[...]).astype(BF16)
    k_pe = p[MLA_Q_RANK + MLA_KV_RANK:MLA_Q_RANK + MLA_KV_RANK + rd]
    if rotate:
        k_pe = _rotate_rows(k_pe, rope_ref)
    q = jnp.dot(wuq_ref[0], cq, preferred_element_type=F32)
    kv = jnp.dot(wukv_ref[0], ckv, preferred_element_type=F32)
    q_scale = MLA_QK_DIM ** -0.5 * LOG2E
    zeros = jnp.zeros((pad - nope - rd, t), F32)
    for i in range(nh):
        qh = q[i * pad:(i + 1) * pad]
        q_pe = qh[nope:nope + rd]
        if rotate:
            q_pe = _rotate_rows(q_pe, rope_ref)
        qh = jnp.concatenate([qh[0:nope], q_pe, qh[nope + rd:pad]], axis=0)
        q_ref[0, i * pad:(i + 1) * pad, :] = (qh * q_scale).astype(BF16)
        kh = jnp.concatenate([kv[i * nope:(i + 1) * nope], k_pe, zeros], axis=0)
        k_ref[0, i] = kh.T.astype(BF16)
    _store_values(v_ref, kv[nh * nope:], nh)


def _mla_proj(x_t, modc, layer, mod_index, w_idx, win_t, gq_col, gkv_col, wuq_t, wukv_t, rope, tile):
    b, d, n = x_t.shape
    nh, pad = MLA_HEADS, MLA_QK_PAD
    rotate = rope is not None
    in_specs = [pl.BlockSpec((1, d, tile), lambda bi, ti: (bi, 0, ti)),
                pl.BlockSpec((1, 1, 2 * d, 1), lambda bi, ti: (layer, mod_index(bi), 0, 0)),
                _layer_spec(win_t, w_idx),
                pl.BlockSpec(gq_col.shape, lambda bi, ti: (0, 0)),
                pl.BlockSpec(gkv_col.shape, lambda bi, ti: (0, 0)),
                _layer_spec(wuq_t, w_idx),
                _layer_spec(wukv_t, w_idx)]
    args = [x_t, modc, win_t, gq_col, gkv_col, wuq_t, wukv_t]
    if rotate:
        in_specs.append(pl.BlockSpec((4, MLA_ROPE_DIM // 4, tile), lambda bi, ti: (0, 0, ti)))
        args.append(rope)
    return pl.pallas_call(
        functools.partial(_mla_proj_kernel, rotate=rotate),
        grid=(b, n // tile),
        in_specs=in_specs,
        out_specs=[pl.BlockSpec((1, nh * pad, tile), lambda bi, ti: (bi, 0, ti)),
                   pl.BlockSpec((1, nh, tile, pad), lambda bi, ti: (bi, 0, ti, 0)),
                   pl.BlockSpec((1, nh * V_ROWS, tile), lambda bi, ti: (bi, 0, ti))],
        out_shape=[jax.ShapeDtypeStruct((b, nh * pad, n), BF16),
                   jax.ShapeDtypeStruct((b, nh, n, pad), BF16),
                   jax.ShapeDtypeStruct((b, nh * V_ROWS, n), BF16)],
        compiler_params=_params(2),
        name="mla_proj_lat" if rotate else "mla_proj_ctx",
    )(*args)


def _attn_kernel(q_ref, *rest, n_heads, group, dk_rows, dv):
    n_src = (len(rest) - 5) // 3
    k_refs = rest[0:2 * n_src:2]
    v_refs = rest[1:2 * n_src:2]
    qn_ref = rest[2 * n_src]
    kn_refs = rest[2 * n_src + 1:3 * n_src + 1]
    o_ref = rest[3 * n_src + 1]
    s_refs = rest[3 * n_src + 2:3 * n_src + 4]
    m_ref = rest[3 * n_src + 4]
    chunks = []
    base = 0
    for src, k_ref in enumerate(k_refs):
        for lo in range(0, k_ref.shape[2], KV_CHUNK):
            chunks.append((src, lo, base + lo))
        base += k_ref.shape[2]

    def head_of_this_step(h, slot):
        kvh = h // group
        q = q_ref[0, pl.ds(pl.multiple_of(h * dk_rows, dk_rows), dk_rows), :]
        return q, (lambda src, lo: k_refs[src][0, kvh, lo:lo + KV_CHUNK, :]), slot

    def head_0_of_next_step():
        return qn_ref[0], (lambda src, lo: kn_refs[src][0, 0, lo:lo + KV_CHUNK, :]), 0

    def step(cur, nxt):
        q, load_keys, slot_n = nxt
        if cur is not None:
            h_c, slot_c, m_c = cur
            v_row = pl.multiple_of((h_c // group) * V_ROWS, ONES_ROWS)
        m_n = None
        acc = None
        for src, lo, glo in chunks:
            s = jnp.dot(load_keys(src, lo), q, preferred_element_type=F32)
            s_refs[slot_n][glo:glo + KV_CHUNK, :] = s
            ms = jnp.max(s, axis=0, keepdims=True)
            m_n = ms if m_n is None else jnp.maximum(m_n, ms)
            if cur is not None:
                p = jnp.exp2(s_refs[slot_c][glo:glo + KV_CHUNK, :] - m_c).astype(BF16)
                part = jnp.dot(v_refs[src][0, pl.ds(v_row, V_ROWS), lo:lo + KV_CHUNK], p,
                               preferred_element_type=F32)
                acc = part if acc is None else acc + part
        if cur is not None:
            out = acc[0:dv] / acc[dv:dv + 1]
            o_ref[0, pl.ds(pl.multiple_of(h_c * dv, dv), dv), :] = out.astype(BF16)
        return m_n

    @pl.when(jnp.logical_and(pl.program_id(0) == 0, pl.program_id(1) == 0))
    def _():
        m_ref[...] = step(None, head_of_this_step(0, 0))

    def pair(i, m_even):
        h0 = 2 * i
        m_odd = step((h0, 0, m_even), head_of_this_step(h0 + 1, 1))
        return step((h0 + 1, 1, m_odd), head_of_this_step(h0 + 2, 0))

    m_even = lax.fori_loop(0, n_heads // 2 - 1, pair, m_ref[...])
    m_odd = step((n_heads - 2, 0, m_even), head_of_this_step(n_heads - 1, 1))
    m_ref[...] = step((n_heads - 1, 1, m_odd), head_0_of_next_step())


def _attention(q_t, kv_sources, batch, n_heads, group, dk_rows, dv, tq, name):
    qb, qrows, nq = q_t.shape
    tiles = 1 if qb == 1 else nq // tq
    grid = (batch, tiles)

    def next_step(bi, ti):
        flat = jnp.minimum(bi * tiles + ti + 1, batch * tiles - 1)
        return flat // tiles, flat % tiles

    if qb == 1:
        q_index = lambda bi, ti: (0, 0, bi)
        qn_index = lambda bi, ti: (0, 0, next_step(bi, ti)[0])
    else:
        q_index = lambda bi, ti: (bi, 0, ti)
        qn_index = lambda bi, ti: (next_step(bi, ti)[0], 0, next_step(bi, ti)[1])
    in_specs = [pl.BlockSpec((1, qrows, tq), q_index)]
    args = [q_t]
    next_specs = [pl.BlockSpec((1, dk_rows, tq), qn_index)]
    next_args = [q_t]
    n_keys = 0
    for k, v, keys in kv_sources:
        if k.shape[0] == 1:
            k_index, v_index = (lambda bi, ti: (0, 0, bi, 0)), (lambda bi, ti: (0, 0, bi))
            kn_index = lambda bi, ti: (0, 0, next_step(bi, ti)[0], 0)
        else:
            k_index, v_index = (lambda bi, ti: (bi, 0, 0, 0)), (lambda bi, ti: (bi, 0, 0))
            kn_index = lambda bi, ti: (next_step(bi, ti)[0], 0, 0, 0)
        in_specs.append(pl.BlockSpec((1, k.shape[1], keys, k.shape[3]), k_index))
        in_specs.append(pl.BlockSpec((1, v.shape[1], keys), v_index))
        args += [k, v]
        next_specs.append(pl.BlockSpec((1, 1, keys, k.shape[3]), kn_index))
        next_args.append(k)
        n_keys += keys
    return pl.pallas_call(
        functools.partial(_attn_kernel, n_heads=n_heads, group=group, dk_rows=dk_rows, dv=dv),
        grid=grid,
        in_specs=in_specs + next_specs,
        out_specs=pl.BlockSpec((1, n_heads * dv, tq), q_index),
        out_shape=jax.ShapeDtypeStruct((qb, n_heads * dv, nq), BF16),
        scratch_shapes=[pltpu.VMEM((n_keys, tq), F32), pltpu.VMEM((n_keys, tq), F32),
                        pltpu.VMEM((1, tq), F32)],
        compiler_params=_params(2),
        name=name,
    )(*args, *next_args)


def _post_kernel(x_ref, a_ref, moda_ref, modm_ref, ln_ref, wo_ref, w1_ref, w2_ref, o_ref, *, natural_out):
    d = D_MODEL
    t = x_ref.shape[2]
    half = t // 2
    pieces = half // LANES
    n_chunks = FFN_HIDDEN // FFN_CHUNK
    gate_a = moda_ref[0, 0, 0:d, :]
    shift_m = moda_ref[0, 0, d:2 * d, :]
    scale_m = modm_ref[0, 0, 0:d, :]
    gate_m = modm_ref[0, 0, d:2 * d, :]
    ln = [ln_ref[0, i] for i in range(4)]

    def merge_attn(lo, width, y):
        x1 = _layer_norm_rows(DEEPNORM_ALPHA * x_ref[0, :, lo:lo + width] + gate_a * y, ln[0], ln[1])
        return x1, (x1 * (1.0 + scale_m) + shift_m).astype(BF16)

    def mlp_chunk(c, h):
        u = jnp.dot(w1_ref[0, c * FFN_CHUNK:(c + 1) * FFN_CHUNK, :], h, preferred_element_type=F32)
        u = jnp.maximum(u, 0.0)
        u = (u * u).astype(BF16)
        return jnp.dot(w2_ref[0, :, c * FFN_CHUNK:(c + 1) * FFN_CHUNK], u, preferred_element_type=F32)

    def merge_mlp(lo, x1, acc):
        x2 = _layer_norm_rows(DEEPNORM_ALPHA * x1 + gate_m * acc, ln[2], ln[3])
        if natural_out:
            o_ref[0, lo:lo + LANES, :] = x2.T
        else:
            o_ref[0, :, lo:lo + LANES] = x2

    y_a = jnp.dot(wo_ref[0], a_ref[0, :, 0:half], preferred_element_type=F32)
    y_b = jnp.dot(wo_ref[0], a_ref[0, :, half:t], preferred_element_type=F32)
    x1_a, h_a = merge_attn(0, half, y_a)
    acc_a = None
    x1_b, h_b = [], []
    for c in range(n_chunks):
        part = mlp_chunk(c, h_a)
        acc_a = part if acc_a is None else acc_a + part
        if c < pieces:
            x1, h = merge_attn(half + c * LANES, LANES, y_b[:, c * LANES:(c + 1) * LANES])
            x1_b.append(x1)
            h_b.append(h)
    h_b = jnp.concatenate(h_b, axis=1)
    acc_b = None
    for c in range(n_chunks):
        part = mlp_chunk(c, h_b)
        acc_b = part if acc_b is None else acc_b + part
        if c < pieces:
            merge_mlp(c * LANES, x1_a[:, c * LANES:(c + 1) * LANES], acc_a[:, c * LANES:(c + 1) * LANES])
    for j in range(pieces):
        merge_mlp(half + j * LANES, x1_b[j], acc_b[:, j * LANES:(j + 1) * LANES])


def _post(x_t, attn_t, modc, ln_cols, layer, mod_index, wo_t, wo_idx, w1_t, w2_t, tile, natural_out, name):
    b, d, n = x_t.shape
    single = dict(pipeline_mode=pl.Buffered(1))
    if natural_out:
        out_spec = pl.BlockSpec((1, tile, d), lambda bi, ti: (bi, ti, 0))
        out_shape = jax.ShapeDtypeStruct((b, n, d), F32)
    else:
        out_spec = pl.BlockSpec((1, d, tile), lambda bi, ti: (bi, 0, ti))
        out_shape = jax.ShapeDtypeStruct((b, d, n), F32)
    return pl.pallas_call(
        functools.partial(_post_kernel, natural_out=natural_out),
        grid=(b, n // tile),
        in_specs=[pl.BlockSpec((1, d, tile), lambda bi, ti: (bi, 0, ti)),
                  pl.BlockSpec((1, d, tile), lambda bi, ti: (bi, 0, ti)),
                  pl.BlockSpec((1, 1, 2 * d, 1), lambda bi, ti: (layer, mod_index(bi), 1, 0)),
                  pl.BlockSpec((1, 1, 2 * d, 1), lambda bi, ti: (layer, mod_index(bi), 2, 0)),
                  pl.BlockSpec((1, 4, d, 1), lambda bi, ti: (layer, 0, 0, 0)),
                  _layer_spec(wo_t, wo_idx, **single),
                  _layer_spec(w1_t, layer, **single),
                  _layer_spec(w2_t, layer, **single)],
        out_specs=out_spec,
        out_shape=out_shape,
        compiler_params=_params(2),
        name=name,
    )(x_t, attn_t, modc, modc, ln_cols, wo_t, w1_t, w2_t)


def _rope_tables_t(n_tok, rot_dim):
    rows = n_tok // GRID_W
    row = jnp.broadcast_to(jnp.arange(rows, dtype=F32)[:, None], (rows, GRID_W)).reshape(-1)
    col = jnp.broadcast_to(jnp.arange(GRID_W, dtype=F32)[None, :], (rows, GRID_W)).reshape(-1)
    axis_dim = rot_dim // 2
    inv_freq = ROPE_THETA ** (-jnp.arange(0, axis_dim, 2, dtype=F32) / axis_dim)
    ang_row = inv_freq[:, None] * row[None, :]
    ang_col = inv_freq[:, None] * col[None, :]
    return jnp.stack([jnp.cos(ang_row), jnp.sin(ang_row), jnp.cos(ang_col), jnp.sin(ang_col)])


def _col(v):
    return v.astype(F32)[:, None]


def kernel(x, c, ctx, c_ctx, w_ada, b_ada, ln_g, ln_b, mlp_w1, mlp_w2, gqa_w_qkv, gqa_q_norm, gqa_k_norm, gqa_w_o, mla_w_in, mla_q_norm, mla_kv_norm, mla_w_uq, mla_w_ukv, mla_w_o):
    b, n_lat, d = x.shape
    n_ctx = ctx.shape[1]

    cv = jnp.concatenate([c, c_ctx[None, :], jnp.zeros((MOD_ROWS - b - 1, d), F32)], axis=0)
    mod = _modulation(cv, w_ada, b_ada)
    modc = mod[:, :b + 1, :, None]
    ln_cols = jnp.stack([ln_g[:, 0], ln_b[:, 0], ln_g[:, 1], ln_b[:, 1]], axis=1)[..., None]

    lat_mod = lambda bi: bi
    ctx_mod = lambda bi: b

    rope_gqa = _rope_tables_t(n_lat, GQA_HEAD_DIM)
    rope_mla = _rope_tables_t(n_lat, MLA_ROPE_DIM)

    w1_t, w2_t = _transpose_cast(mlp_w1), _transpose_cast(mlp_w2)
    gqa_w_t, gqa_wo_t = _transpose_cast(gqa_w_qkv), _transpose_cast(gqa_w_o)
    mla_wo_t = _transpose_cast(mla_w_o)
    n_b = mla_w_in.shape[0]
    mla_win_t = mla_w_in.swapaxes(1, 2).astype(BF16)
    wuq = mla_w_uq.reshape(n_b, MLA_Q_RANK, MLA_HEADS, MLA_QK_DIM)
    wuq = jnp.pad(wuq, ((0, 0), (0, 0), (0, 0), (0, MLA_QK_PAD - MLA_QK_DIM)))
    mla_wuq_t = wuq.reshape(n_b, MLA_Q_RANK, MLA_HEADS * MLA_QK_PAD).swapaxes(1, 2).astype(BF16)
    wukv = mla_w_ukv.reshape(n_b, MLA_KV_RANK, MLA_HEADS, MLA_NOPE_DIM + MLA_V_DIM)
    wukv = jnp.concatenate([wukv[..., :MLA_NOPE_DIM].reshape(n_b, MLA_KV_RANK, -1),
                            wukv[..., MLA_NOPE_DIM:].reshape(n_b, MLA_KV_RANK, -1)], axis=2)
    mla_wukv_t = wukv.swapaxes(1, 2).astype(BF16)

    xc_nat = ctx.reshape(1, b * n_ctx, d)
    x_t = xc_t = None

    for i in range(DEPTH):
        need_ctx = i < DEPTH - 1
        last = i == DEPTH - 1
        j = i // N_MIXERS
        if i % N_MIXERS == 0:
            wo_t = gqa_wo_t
            gq, gk = _col(gqa_q_norm[j]), _col(gqa_k_norm[j])
            if i == 0:
                q_l, k_l, v_l, x_t = _gqa_proj(x, True, modc, i, lat_mod, gqa_w_t, j, gq, gk, rope_gqa, LAT_TILE)
                q_c, k_c, v_c, xc_t = _gqa_proj(xc_nat, True, modc, i, ctx_mod, gqa_w_t, j, gq, gk, None, LAT_TILE)
            else:
                q_l, k_l, v_l = _gqa_proj(x_t, False, modc, i, lat_mod, gqa_w_t, j, gq, gk, rope_gqa, LAT_TILE)
                q_c, k_c, v_c = _gqa_proj(xc_t, False, modc, i, ctx_mod, gqa_w_t, j, gq, gk, None, LAT_TILE)
            heads, group, dk_rows, dv = GQA_HEADS, GQA_HEADS // GQA_KV_HEADS, GQA_HEAD_DIM, GQA_HEAD_DIM
        else:
            wo_t = mla_wo_t
            gq, gkv = _col(mla_q_norm[j]), _col(mla_kv_norm[j])
            q_l, k_l, v_l = _mla_proj(x_t, modc, i, lat_mod, j, mla_win_t, gq, gkv, mla_wuq_t, mla_wukv_t,
                                      rope_mla, LAT_TILE)
            q_c, k_c, v_c = _mla_proj(xc_t, modc, i, ctx_mod, j, mla_win_t, gq, gkv, mla_wuq_t, mla_wukv_t,
                                      None, LAT_TILE)
            heads, group, dk_rows, dv = MLA_HEADS, 1, MLA_QK_PAD, MLA_V_DIM

        a_l = _attention(q_l, [(k_c, v_c, n_ctx), (k_l, v_l, n_lat)], b, heads, group, dk_rows, dv, LAT_TILE,
                         "attn_lat")
        x_t = _post(x_t, a_l, modc, ln_cols, i, lat_mod, wo_t, j, w1_t, w2_t, POST_TILE, last, "post_lat")
        if need_ctx:
            a_c = _attention(q_c, [(k_c, v_c, n_ctx)], b, heads, group, dk_rows, dv, n_ctx, "attn_ctx")
            xc_t = _post(xc_t, a_c, modc, ln_cols, i, ctx_mod, wo_t, j, w1_t, w2_t, POST_TILE, False, "post_ctx")

    return x_t
```

```python
import functools

import jax
import jax.numpy as jnp
from jax import lax
from jax.experimental import pallas as pl
from jax.experimental.pallas import tpu as pltpu

D_MODEL = 1024
DEPTH = 4
GRID_W = 64
N_MIXERS = 2
GQA_HEADS = 16
GQA_KV_HEADS = 4
GQA_HEAD_DIM = 64
MLA_HEADS = 16
MLA_Q_RANK = 384
MLA_KV_RANK = 256
MLA_NOPE_DIM = 64
MLA_ROPE_DIM = 32
MLA_V_DIM = 64
MLA_QK_DIM = MLA_NOPE_DIM + MLA_ROPE_DIM
MLA_QK_PAD = 128
FFN_HIDDEN = 4 * D_MODEL
ROPE_THETA = 10000.0
NORM_EPS = 1e-6
DEEPNORM_ALPHA = (2.0 * DEPTH) ** 0.25
LOG2E = 1.4426950408889634

HEAD_V_DIM = 64
ONES_ROWS = 16
V_ROWS = HEAD_V_DIM + ONES_ROWS

LAT_TILE = 512
POST_TILE = 1024
KV_CHUNK = 256
HEADS_PER_LOOP = 4
FFN_CHUNK = 1024
WEIGHT_TILE = 1024
MOD_ROWS = 16
PROJ_ROWS = 256
LANES = 128
VMEM_LIMIT_BYTES = 60 * 1024 * 1024

F32 = jnp.float32
BF16 = jnp.bfloat16


def _params(n_axes):
    return pltpu.CompilerParams(dimension_semantics=("arbitrary",) * n_axes,
                                vmem_limit_bytes=VMEM_LIMIT_BYTES)


def _transpose_cast_kernel(w_ref, o_ref):
    o_ref[0] = w_ref[0].T.astype(BF16)


def _transpose_cast(w):
    n_l, k, m = w.shape
    tk = WEIGHT_TILE if k % WEIGHT_TILE == 0 else WEIGHT_TILE // 2
    tm = WEIGHT_TILE if m % WEIGHT_TILE == 0 else WEIGHT_TILE // 2
    return pl.pallas_call(
        _transpose_cast_kernel,
        grid=(n_l, k // tk, m // tm),
        in_specs=[pl.BlockSpec((1, tk, tm), lambda l, i, j: (l, i, j))],
        out_specs=pl.BlockSpec((1, tm, tk), lambda l, i, j: (l, j, i)),
        out_shape=jax.ShapeDtypeStruct((n_l, m, k), BF16),
        compiler_params=_params(3),
        name="weight_transpose_cast",
    )(w)


def _layer_spec(w, idx, **kw):
    return pl.BlockSpec((1,) + w.shape[1:], lambda bi, ti: (idx, 0, 0), **kw)


def _mod_kernel(cv_ref, w_ref, b_ref, o_ref):
    c = cv_ref[...]
    s = c * jax.nn.sigmoid(c)
    w = w_ref[0]
    s_hi = s.astype(BF16)
    s_lo = (s - s_hi.astype(F32)).astype(BF16)
    w_hi = w.astype(BF16)
    w_lo = (w - w_hi.astype(F32)).astype(BF16)
    dot = functools.partial(jnp.dot, preferred_element_type=F32)
    o_ref[0] = dot(s_hi, w_hi) + (dot(s_lo, w_hi) + dot(s_hi, w_lo)) + b_ref[0]


def _modulation(cv, w_ada, b_ada):
    d = D_MODEL
    return pl.pallas_call(
        _mod_kernel,
        grid=(DEPTH, 6),
        in_specs=[pl.BlockSpec((MOD_ROWS, d), lambda i, j: (0, 0)),
                  pl.BlockSpec((1, d, d), lambda i, j: (i, 0, j)),
                  pl.BlockSpec((1, 1, d), lambda i, j: (i, 0, j))],
        out_specs=pl.BlockSpec((1, MOD_ROWS, d), lambda i, j: (i, 0, j)),
        out_shape=jax.ShapeDtypeStruct((DEPTH, MOD_ROWS, 6 * d), F32),
        compiler_params=_params(2),
        name="adaln_modulation",
    )(cv, w_ada, b_ada.reshape(DEPTH, 1, 6 * d))


def _modulate(x, mod_ref):
    d = D_MODEL
    shift = mod_ref[0, 0, 0:d, :]
    scale = mod_ref[0, 0, d:2 * d, :]
    return x * (1.0 + scale) + shift


def _rms_rows(x, gain_col):
    ms = jnp.mean(x * x, axis=0, keepdims=True)
    return x * lax.rsqrt(ms + NORM_EPS) * gain_col


def _layer_norm_rows(z, g_col, b_col):
    mu = jnp.mean(z, axis=0, keepdims=True)
    zc = z - mu
    var = jnp.mean(zc * zc, axis=0, keepdims=True)
    return zc * lax.rsqrt(var + NORM_EPS) * g_col + b_col


def _rotate_rows(x, rope_ref):
    q = x.shape[0] // 4
    cr, sr, cc, sc = rope_ref[0], rope_ref[1], rope_ref[2], rope_ref[3]
    a1, a2, b1, b2 = x[0:q], x[q:2 * q], x[2 * q:3 * q], x[3 * q:4 * q]
    return jnp.concatenate([a1 * cr - a2 * sr, a2 * cr + a1 * sr,
                            b1 * cc - b2 * sc, b2 * cc + b1 * sc], axis=0)


def _store_values(v_ref, v, n_heads):
    ones = jnp.ones((ONES_ROWS, v.shape[1]), BF16)
    for j in range(n_heads):
        v_ref[0, j * V_ROWS:j * V_ROWS + HEAD_V_DIM, :] = v[j * HEAD_V_DIM:(j + 1) * HEAD_V_DIM].astype(BF16)
        v_ref[0, j * V_ROWS + HEAD_V_DIM:(j + 1) * V_ROWS, :] = ones


def _gqa_proj_kernel(x_ref, mod_ref, w_ref, gq_ref, gk_ref, *rest, rotate, natural_in):
    rest = list(rest)
    rope_ref = rest.pop(0) if rotate else None
    q_ref, k_ref, v_ref = rest[:3]
    hd = GQA_HEAD_DIM
    nq = GQA_HEADS * hd
    nk = GQA_KV_HEADS * hd
    if natural_in:
        x = x_ref[0].T
        rest[3][0] = x
    else:
        x = x_ref[0]
    h = _modulate(x, mod_ref).astype(BF16)
    gq = gq_ref[...]
    gk = gk_ref[...]
    q_scale = hd ** -0.5 * LOG2E
    heads_per_step = PROJ_ROWS // hd

    def project(row0, rows):
        return jnp.dot(w_ref[0, row0:row0 + rows, :], h, preferred_element_type=F32)

    for g in range(nq // PROJ_ROWS):
        p = project(g * PROJ_ROWS, PROJ_ROWS)
        for i in range(heads_per_step):
            qh = _rms_rows(p[i * hd:(i + 1) * hd], gq)
            if rotate:
                qh = _rotate_rows(qh, rope_ref)
            row = g * PROJ_ROWS + i * hd
            q_ref[0, row:row + hd, :] = (qh * q_scale).astype(BF16)
    p = project(nq, nk)
    ks = []
    for j in range(GQA_KV_HEADS):
        kh = _rms_rows(p[j * hd:(j + 1) * hd], gk)
        if rotate:
            kh = _rotate_rows(kh, rope_ref)
        ks.append(kh)
    kt = jnp.concatenate(ks, axis=0).T
    for j in range(GQA_KV_HEADS):
        k_ref[0, j] = kt[:, j * hd:(j + 1) * hd].astype(BF16)
    _store_values(v_ref, project(nq + nk, nk), GQA_KV_HEADS)


def _gqa_proj(x, natural_in, modc, layer, mod_index, w_t, w_idx, gq_col, gk_col, rope, tile):
    if natural_in:
        b, n, d = x.shape
        x_spec = pl.BlockSpec((1, tile, d), lambda bi, ti: (bi, ti, 0))
    else:
        b, d, n = x.shape
        x_spec = pl.BlockSpec((1, d, tile), lambda bi, ti: (bi, 0, ti))
    hd = GQA_HEAD_DIM
    nq = GQA_HEADS * hd
    rotate = rope is not None
    in_specs = [x_spec,
                pl.BlockSpec((1, 1, 2 * d, 1), lambda bi, ti: (layer, mod_index(bi), 0, 0)),
                _layer_spec(w_t, w_idx),
                pl.BlockSpec(gq_col.shape, lambda bi, ti: (0, 0)),
                pl.BlockSpec(gk_col.shape, lambda bi, ti: (0, 0))]
    args = [x, modc, w_t, gq_col, gk_col]
    if rotate:
        in_specs.append(pl.BlockSpec((4, hd // 4, tile), lambda bi, ti: (0, 0, ti)))
        args.append(rope)
    out_specs = [pl.BlockSpec((1, nq, tile), lambda bi, ti: (bi, 0, ti)),
                 pl.BlockSpec((1, GQA_KV_HEADS, tile, hd), lambda bi, ti: (bi, 0, ti, 0)),
                 pl.BlockSpec((1, GQA_KV_HEADS * V_ROWS, tile), lambda bi, ti: (bi, 0, ti))]
    out_shape = [jax.ShapeDtypeStruct((b, nq, n), BF16),
                 jax.ShapeDtypeStruct((b, GQA_KV_HEADS, n, hd), BF16),
                 jax.ShapeDtypeStruct((b, GQA_KV_HEADS * V_ROWS, n), BF16)]
    if natural_in:
        out_specs.append(pl.BlockSpec((1, d, tile), lambda bi, ti: (bi, 0, ti)))
        out_shape.append(jax.ShapeDtypeStruct((b, d, n), F32))
    return pl.pallas_call(
        functools.partial(_gqa_proj_kernel, rotate=rotate, natural_in=natural_in),
        grid=(b, n // tile),
        in_specs=in_specs,
        out_specs=out_specs,
        out_shape=out_shape,
        compiler_params=_params(2),
        name="gqa_proj_lat" if rotate else "gqa_proj_ctx",
    )(*args)


def _mla_proj_kernel(x_ref, mod_ref, win_ref, gq_ref, gkv_ref, wuq_ref, wukv_ref, *rest, rotate):
    if rotate:
        rope_ref, q_ref, k_ref, v_ref = rest
    else:
        q_ref, k_ref, v_ref = rest
    nh, nope, rd, pad = MLA_HEADS, MLA_NOPE_DIM, MLA_ROPE_DIM, MLA_QK_PAD
    t = x_ref.shape[2]
    h = _modulate(x_ref[0], mod_ref).astype(BF16)
    p = jnp.dot(win_ref[0], h, preferred_element_type=F32)
    cq = _rms_rows(p[0:MLA_Q_RANK], gq_ref[...]).astype(BF16)
    ckv = _rms_rows(p[MLA_Q_RANK:MLA_Q_RANK + MLA_KV_RANK], gkv_ref[...]).astype(BF16)
    k_pe = p[MLA_Q_RANK + MLA_KV_RANK:MLA_Q_RANK + MLA_KV_RANK + rd]
    if rotate:
        k_pe = _rotate_rows(k_pe, rope_ref)
    q = jnp.dot(wuq_ref[0], cq, preferred_element_type=F32)
    kv = jnp.dot(wukv_ref[0], ckv, preferred_element_type=F32)
    q_scale = MLA_QK_DIM ** -0.5 * LOG2E
    zeros = jnp.zeros((pad - nope - rd, t), F32)
    qk = MLA_QK_DIM
    for i in range(nh):
        q_pe = q[i * qk + nope:(i + 1) * qk]
        if rotate:
            q_pe = _rotate_rows(q_pe, rope_ref)
        qh = jnp.concatenate([q[i * qk:i * qk + nope], q_pe], axis=0) * q_scale
        q_ref[0, i * pad:(i + 1) * pad, :] = jnp.concatenate([qh, zeros], axis=0).astype(BF16)
        kh = jnp.concatenate([kv[i * nope:(i + 1) * nope], k_pe, zeros], axis=0)
        k_ref[0, i] = kh.T.astype(BF16)
    _store_values(v_ref, kv[nh * nope:], nh)


def _mla_proj(x_t, modc, layer, mod_index, w_idx, win_t, gq_col, gkv_col, wuq_t, wukv_t, rope, tile):
    b, d, n = x_t.shape
    nh, pad = MLA_HEADS, MLA_QK_PAD
    rotate = rope is not None
    in_specs = [pl.BlockSpec((1, d, tile), lambda bi, ti: (bi, 0, ti)),
                pl.BlockSpec((1, 1, 2 * d, 1), lambda bi, ti: (layer, mod_index(bi), 0, 0)),
                _layer_spec(win_t, w_idx),
                pl.BlockSpec(gq_col.shape, lambda bi, ti: (0, 0)),
                pl.BlockSpec(gkv_col.shape, lambda bi, ti: (0, 0)),
                _layer_spec(wuq_t, w_idx),
                _layer_spec(wukv_t, w_idx)]
    args = [x_t, modc, win_t, gq_col, gkv_col, wuq_t, wukv_t]
    if rotate:
        in_specs.append(pl.BlockSpec((4, MLA_ROPE_DIM // 4, tile), lambda bi, ti: (0, 0, ti)))
        args.append(rope)
    return pl.pallas_call(
        functools.partial(_mla_proj_kernel, rotate=rotate),
        grid=(b, n // tile),
        in_specs=in_specs,
        out_specs=[pl.BlockSpec((1, nh * pad, tile), lambda bi, ti: (bi, 0, ti)),
                   pl.BlockSpec((1, nh, tile, pad), lambda bi, ti: (bi, 0, ti, 0)),
                   pl.BlockSpec((1, nh * V_ROWS, tile), lambda bi, ti: (bi, 0, ti))],
        out_shape=[jax.ShapeDtypeStruct((b, nh * pad, n), BF16),
                   jax.ShapeDtypeStruct((b, nh, n, pad), BF16),
                   jax.ShapeDtypeStruct((b, nh * V_ROWS, n), BF16)],
        compiler_params=_params(2),
        name="mla_proj_lat" if rotate else "mla_proj_ctx",
    )(*args)


def _attn_kernel(q_ref, *rest, n_heads, group, dk_rows, dv):
    n_src = (len(rest) - 5) // 3
    k_refs = rest[0:2 * n_src:2]
    v_refs = rest[1:2 * n_src:2]
    qn_ref = rest[2 * n_src]
    kn_refs = rest[2 * n_src + 1:3 * n_src + 1]
    o_ref = rest[3 * n_src + 1]
    s_refs = rest[3 * n_src + 2:3 * n_src + 4]
    m_ref = rest[3 * n_src + 4]
    chunks = []
    base = 0
    for src, k_ref in enumerate(k_refs):
        for lo in range(0, k_ref.shape[2], KV_CHUNK):
            chunks.append((src, lo, base + lo))
        base += k_ref.shape[2]

    def head_of_this_step(h, slot):
        kvh = h // group
        q = q_ref[0, pl.ds(pl.multiple_of(h * dk_rows, dk_rows), dk_rows), :]
        return q, (lambda src, lo: k_refs[src][0, kvh, lo:lo + KV_CHUNK, :]), slot

    def head_0_of_next_step():
        return qn_ref[0], (lambda src, lo: kn_refs[src][0, 0, lo:lo + KV_CHUNK, :]), 0

    def step(cur, nxt):
        q, load_keys, slot_n = nxt
        if cur is not None:
            h_c, slot_c, m_c = cur
            v_row = pl.multiple_of((h_c // group) * V_ROWS, ONES_ROWS)
        m_n = None
        acc = None
        for src, lo, glo in chunks:
            s = jnp.dot(load_keys(src, lo), q, preferred_element_type=F32)
            s_refs[slot_n][glo:glo + KV_CHUNK, :] = s
            ms = jnp.max(s, axis=0, keepdims=True)
            m_n = ms if m_n is None else jnp.maximum(m_n, ms)
            if cur is not None:
                p = jnp.exp2(s_refs[slot_c][glo:glo + KV_CHUNK, :] - m_c).astype(BF16)
                part = jnp.dot(v_refs[src][0, pl.ds(v_row, V_ROWS), lo:lo + KV_CHUNK], p,
                               preferred_element_type=F32)
                acc = part if acc is None else acc + part
        if cur is not None:
            out = acc[0:dv] / acc[dv:dv + 1]
            o_ref[0, pl.ds(pl.multiple_of(h_c * dv, dv), dv), :] = out.astype(BF16)
        return m_n

    @pl.when(jnp.logical_and(pl.program_id(0) == 0, pl.program_id(1) == 0))
    def _():
        m_ref[...] = step(None, head_of_this_step(0, 0))

    def run(i, m):
        h0 = i * HEADS_PER_LOOP
        for t in range(HEADS_PER_LOOP):
            m = step((h0 + t, t % 2, m), head_of_this_step(h0 + t + 1, (t + 1) % 2))
        return m

    n_loop = n_heads // HEADS_PER_LOOP - 1
    m = lax.fori_loop(0, n_loop, run, m_ref[...])
    h0 = n_loop * HEADS_PER_LOOP
    for t in range(HEADS_PER_LOOP - 1):
        m = step((h0 + t, t % 2, m), head_of_this_step(h0 + t + 1, (t + 1) % 2))
    m_ref[...] = step((n_heads - 1, 1, m), head_0_of_next_step())


def _attention(q_t, kv_sources, batch, n_heads, group, dk_rows, dv, tq, name):
    qb, qrows, nq = q_t.shape
    tiles = 1 if qb == 1 else nq // tq
    grid = (batch, tiles)

    def next_step(bi, ti):
        flat = jnp.minimum(bi * tiles + ti + 1, batch * tiles - 1)
        return flat // tiles, flat % tiles

    if qb == 1:
        q_index = lambda bi, ti: (0, 0, bi)
        qn_index = lambda bi, ti: (0, 0, next_step(bi, ti)[0])
    else:
        q_index = lambda bi, ti: (bi, 0, ti)
        qn_index = lambda bi, ti: (next_step(bi, ti)[0], 0, next_step(bi, ti)[1])
    in_specs = [pl.BlockSpec((1, qrows, tq), q_index)]
    args = [q_t]
    next_specs = [pl.BlockSpec((1, dk_rows, tq), qn_index)]
    next_args = [q_t]
    n_keys = 0
    for k, v, keys in kv_sources:
        if k.shape[0] == 1:
            k_index, v_index = (lambda bi, ti: (0, 0, bi, 0)), (lambda bi, ti: (0, 0, bi))
            kn_index = lambda bi, ti: (0, 0, next_step(bi, ti)[0], 0)
        else:
            k_index, v_index = (lambda bi, ti: (bi, 0, 0, 0)), (lambda bi, ti: (bi, 0, 0))
            kn_index = lambda bi, ti: (next_step(bi, ti)[0], 0, 0, 0)
        in_specs.append(pl.BlockSpec((1, k.shape[1], keys, k.shape[3]), k_index))
        in_specs.append(pl.BlockSpec((1, v.shape[1], keys), v_index))
        args += [k, v]
        next_specs.append(pl.BlockSpec((1, 1, keys, k.shape[3]), kn_index))
        next_args.append(k)
        n_keys += keys
    return pl.pallas_call(
        functools.partial(_attn_kernel, n_heads=n_heads, group=group, dk_rows=dk_rows, dv=dv),
        grid=grid,
        in_specs=in_specs + next_specs,
        out_specs=pl.BlockSpec((1, n_heads * dv, tq), q_index),
        out_shape=jax.ShapeDtypeStruct((qb, n_heads * dv, nq), BF16),
        scratch_shapes=[pltpu.VMEM((n_keys, tq), F32), pltpu.VMEM((n_keys, tq), F32),
                        pltpu.VMEM((1, tq), F32)],
        compiler_params=_params(2),
        name=name,
    )(*args, *next_args)


def _post_kernel(x_ref, a_ref, moda_ref, modm_ref, ln_ref, wo_ref, w1_ref, w2_ref, o_ref, *, natural_out):
    d = D_MODEL
    t = x_ref.shape[2]
    half = t // 2
    pieces = half // LANES
    n_chunks = FFN_HIDDEN // FFN_CHUNK
    gate_a = moda_ref[0, 0, 0:d, :]
    shift_m = moda_ref[0, 0, d:2 * d, :]
    scale_m = modm_ref[0, 0, 0:d, :]
    gate_m = modm_ref[0, 0, d:2 * d, :]
    ln = [ln_ref[0, i] for i in range(4)]

    def merge_attn(lo, width, y):
        x1 = _layer_norm_rows(DEEPNORM_ALPHA * x_ref[0, :, lo:lo + width] + gate_a * y, ln[0], ln[1])
        return x1, (x1 * (1.0 + scale_m) + shift_m).astype(BF16)

    def mlp_chunk(c, h):
        u = jnp.dot(w1_ref[0, c * FFN_CHUNK:(c + 1) * FFN_CHUNK, :], h, preferred_element_type=F32)
        u = jnp.maximum(u, 0.0)
        u = (u * u).astype(BF16)
        return jnp.dot(w2_ref[0, :, c * FFN_CHUNK:(c + 1) * FFN_CHUNK], u, preferred_element_type=F32)

    def merge_mlp(lo, x1, acc):
        x2 = _layer_norm_rows(DEEPNORM_ALPHA * x1 + gate_m * acc, ln[2], ln[3])
        if natural_out:
            o_ref[0, lo:lo + LANES, :] = x2.T
        else:
            o_ref[0, :, lo:lo + LANES] = x2

    y_a = jnp.dot(wo_ref[0], a_ref[0, :, 0:half], preferred_element_type=F32)
    y_b = jnp.dot(wo_ref[0], a_ref[0, :, half:t], preferred_element_type=F32)
    x1_a, h_a = merge_attn(0, half, y_a)
    acc_a = None
    x1_b, h_b = [], []
    for c in range(n_chunks):
        part = mlp_chunk(c, h_a)
        acc_a = part if acc_a is None else acc_a + part
        if c < pieces:
            x1, h = merge_attn(half + c * LANES, LANES, y_b[:, c * LANES:(c + 1) * LANES])
            x1_b.append(x1)
            h_b.append(h)
    h_b = jnp.concatenate(h_b, axis=1)
    acc_b = None
    for c in range(n_chunks):
        part = mlp_chunk(c, h_b)
        acc_b = part if acc_b is None else acc_b + part
        if c < pieces:
            merge_mlp(c * LANES, x1_a[:, c * LANES:(c + 1) * LANES], acc_a[:, c * LANES:(c + 1) * LANES])
    for j in range(pieces):
        merge_mlp(half + j * LANES, x1_b[j], acc_b[:, j * LANES:(j + 1) * LANES])


def _post(x_t, attn_t, modc, ln_cols, layer, mod_index, wo_t, wo_idx, w1_t, w2_t, tile, natural_out, name):
    b, d, n = x_t.shape
    single = dict(pipeline_mode=pl.Buffered(1))
    if natural_out:
        out_spec = pl.BlockSpec((1, tile, d), lambda bi, ti: (bi, ti, 0))
        out_shape = jax.ShapeDtypeStruct((b, n, d), F32)
    else:
        out_spec = pl.BlockSpec((1, d, tile), lambda bi, ti: (bi, 0, ti))
        out_shape = jax.ShapeDtypeStruct((b, d, n), F32)
    return pl.pallas_call(
        functools.partial(_post_kernel, natural_out=natural_out),
        grid=(b, n // tile),
        in_specs=[pl.BlockSpec((1, d, tile), lambda bi, ti: (bi, 0, ti)),
                  pl.BlockSpec((1, d, tile), lambda bi, ti: (bi, 0, ti)),
                  pl.BlockSpec((1, 1, 2 * d, 1), lambda bi, ti: (layer, mod_index(bi), 1, 0)),
                  pl.BlockSpec((1, 1, 2 * d, 1), lambda bi, ti: (layer, mod_index(bi), 2, 0)),
                  pl.BlockSpec((1, 4, d, 1), lambda bi, ti: (layer, 0, 0, 0)),
                  _layer_spec(wo_t, wo_idx, **single),
                  _layer_spec(w1_t, layer, **single),
                  _layer_spec(w2_t, layer, **single)],
        out_specs=out_spec,
        out_shape=out_shape,
        compiler_params=_params(2),
        name=name,
    )(x_t, attn_t, modc, modc, ln_cols, wo_t, w1_t, w2_t)


def _rope_tables_t(n_tok, rot_dim):
    rows = n_tok // GRID_W
    row = jnp.broadcast_to(jnp.arange(rows, dtype=F32)[:, None], (rows, GRID_W)).reshape(-1)
    col = jnp.broadcast_to(jnp.arange(GRID_W, dtype=F32)[None, :], (rows, GRID_W)).reshape(-1)
    axis_dim = rot_dim // 2
    inv_freq = ROPE_THETA ** (-jnp.arange(0, axis_dim, 2, dtype=F32) / axis_dim)
    ang_row = inv_freq[:, None] * row[None, :]
    ang_col = inv_freq[:, None] * col[None, :]
    return jnp.stack([jnp.cos(ang_row), jnp.sin(ang_row), jnp.cos(ang_col), jnp.sin(ang_col)])


def _col(v):
    return v.astype(F32)[:, None]


def kernel(x, c, ctx, c_ctx, w_ada, b_ada, ln_g, ln_b, mlp_w1, mlp_w2, gqa_w_qkv, gqa_q_norm, gqa_k_norm, gqa_w_o, mla_w_in, mla_q_norm, mla_kv_norm, mla_w_uq, mla_w_ukv, mla_w_o):
    b, n_lat, d = x.shape
    n_ctx = ctx.shape[1]

    cv = jnp.concatenate([c, c_ctx[None, :], jnp.zeros((MOD_ROWS - b - 1, d), F32)], axis=0)
    mod = _modulation(cv, w_ada, b_ada)
    modc = mod[:, :b + 1, :, None]
    ln_cols = jnp.stack([ln_g[:, 0], ln_b[:, 0], ln_g[:, 1], ln_b[:, 1]], axis=1)[..., None]

    lat_mod = lambda bi: bi
    ctx_mod = lambda bi: b

    rope_gqa = _rope_tables_t(n_lat, GQA_HEAD_DIM)
    rope_mla = _rope_tables_t(n_lat, MLA_ROPE_DIM)

    w1_t, w2_t = _transpose_cast(mlp_w1), _transpose_cast(mlp_w2)
    gqa_w_t, gqa_wo_t = _transpose_cast(gqa_w_qkv), _transpose_cast(gqa_w_o)
    mla_wo_t = _transpose_cast(mla_w_o)
    n_b = mla_w_in.shape[0]
    mla_win_t = mla_w_in.swapaxes(1, 2).astype(BF16)
    mla_wuq_t = mla_w_uq.swapaxes(1, 2).astype(BF16)
    wukv = mla_w_ukv.reshape(n_b, MLA_KV_RANK, MLA_HEADS, MLA_NOPE_DIM + MLA_V_DIM)
    wukv = jnp.concatenate([wukv[..., :MLA_NOPE_DIM].reshape(n_b, MLA_KV_RANK, -1),
                            wukv[..., MLA_NOPE_DIM:].reshape(n_b, MLA_KV_RANK, -1)], axis=2)
    mla_wukv_t = wukv.swapaxes(1, 2).astype(BF16)

    xc_nat = ctx.reshape(1, b * n_ctx, d)
    x_t = xc_t = None

    for i in range(DEPTH):
        need_ctx = i < DEPTH - 1
        last = i == DEPTH - 1
        j = i // N_MIXERS
        if i % N_MIXERS == 0:
            wo_t = gqa_wo_t
            gq, gk = _col(gqa_q_norm[j]), _col(gqa_k_norm[j])
            if i == 0:
                q_l, k_l, v_l, x_t = _gqa_proj(x, True, modc, i, lat_mod, gqa_w_t, j, gq, gk, rope_gqa, LAT_TILE)
                q_c, k_c, v_c, xc_t = _gqa_proj(xc_nat, True, modc, i, ctx_mod, gqa_w_t, j, gq, gk, None, LAT_TILE)
            else:
                q_l, k_l, v_l = _gqa_proj(x_t, False, modc, i, lat_mod, gqa_w_t, j, gq, gk, rope_gqa, LAT_TILE)
                q_c, k_c, v_c = _gqa_proj(xc_t, False, modc, i, ctx_mod, gqa_w_t, j, gq, gk, None, LAT_TILE)
            heads, group, dk_rows, dv = GQA_HEADS, GQA_HEADS // GQA_KV_HEADS, GQA_HEAD_DIM, GQA_HEAD_DIM
        else:
            wo_t = mla_wo_t
            gq, gkv = _col(mla_q_norm[j]), _col(mla_kv_norm[j])
            q_l, k_l, v_l = _mla_proj(x_t, modc, i, lat_mod, j, mla_win_t, gq, gkv, mla_wuq_t, mla_wukv_t,
                                      rope_mla, LAT_TILE)
            q_c, k_c, v_c = _mla_proj(xc_t, modc, i, ctx_mod, j, mla_win_t, gq, gkv, mla_wuq_t, mla_wukv_t,
                                      None, LAT_TILE)
            heads, group, dk_rows, dv = MLA_HEADS, 1, MLA_QK_PAD, MLA_V_DIM

        a_l = _attention(q_l, [(k_c, v_c, n_ctx), (k_l, v_l, n_lat)], b, heads, group, dk_rows, dv, LAT_TILE,
                         "attn_lat")
        x_t = _post(x_t, a_l, modc, ln_cols, i, lat_mod, wo_t, j, w1_t, w2_t, POST_TILE, last, "post_lat")
        if need_ctx:
            a_c = _attention(q_c, [(k_c, v_c, n_ctx)], b, heads, group, dk_rows, dv, n_ctx, "attn_ctx")
            xc_t = _post(xc_t, a_c, modc, ln_cols, i, ctx_mod, wo_t, j, w1_t, w2_t, POST_TILE, False, "post_ctx")

    return x_t
```

```python
import functools

import jax
import jax.numpy as jnp
from jax import lax
from jax.experimental import pallas as pl
from jax.experimental.pallas import tpu as pltpu

D_MODEL = 1024
DEPTH = 4
GRID_W = 64
N_MIXERS = 2
GQA_HEADS = 16
GQA_KV_HEADS = 4
GQA_HEAD_DIM = 64
MLA_HEADS = 16
MLA_Q_RANK = 384
MLA_KV_RANK = 256
MLA_NOPE_DIM = 64
MLA_ROPE_DIM = 32
MLA_V_DIM = 64
MLA_QK_DIM = MLA_NOPE_DIM + MLA_ROPE_DIM
MLA_QK_PAD = 128
FFN_HIDDEN = 4 * D_MODEL
ROPE_THETA = 10000.0
NORM_EPS = 1e-6
DEEPNORM_ALPHA = (2.0 * DEPTH) ** 0.25
LOG2E = 1.4426950408889634

HEAD_V_DIM = 64
ONES_ROWS = 16
V_ROWS = HEAD_V_DIM + ONES_ROWS

LAT_TILE = 512
POST_TILE = 1024
KV_CHUNK = 256
HEADS_PER_LOOP = 4
FFN_CHUNK = 1024
WEIGHT_TILE = 1024
MOD_ROWS = 16
PROJ_ROWS = 256
LANES = 128
VMEM_LIMIT_BYTES = 60 * 1024 * 1024

F32 = jnp.float32
BF16 = jnp.bfloat16


def _params(n_axes):
    return pltpu.CompilerParams(dimension_semantics=("arbitrary",) * n_axes,
                                vmem_limit_bytes=VMEM_LIMIT_BYTES)


def _transpose_cast_kernel(w_ref, o_ref):
    o_ref[0] = w_ref[0].T.astype(BF16)


def _transpose_cast(w):
    n_l, k, m = w.shape
    tk = WEIGHT_TILE if k % WEIGHT_TILE == 0 else WEIGHT_TILE // 2
    tm = WEIGHT_TILE if m % WEIGHT_TILE == 0 else WEIGHT_TILE // 2
    return pl.pallas_call(
        _transpose_cast_kernel,
        grid=(n_l, k // tk, m // tm),
        in_specs=[pl.BlockSpec((1, tk, tm), lambda l, i, j: (l, i, j))],
        out_specs=pl.BlockSpec((1, tm, tk), lambda l, i, j: (l, j, i)),
        out_shape=jax.ShapeDtypeStruct((n_l, m, k), BF16),
        compiler_params=_params(3),
        name="weight_transpose_cast",
    )(w)


def _layer_spec(w, idx, **kw):
    return pl.BlockSpec((1,) + w.shape[1:], lambda bi, ti: (idx, 0, 0), **kw)


def _mod_kernel(cv_ref, w_ref, b_ref, o_ref):
    c = cv_ref[...]
    s = c * jax.nn.sigmoid(c)
    w = w_ref[0]
    s_hi = s.astype(BF16)
    s_lo = (s - s_hi.astype(F32)).astype(BF16)
    w_hi = w.astype(BF16)
    w_lo = (w - w_hi.astype(F32)).astype(BF16)
    dot = functools.partial(jnp.dot, preferred_element_type=F32)
    o_ref[0] = dot(s_hi, w_hi) + (dot(s_lo, w_hi) + dot(s_hi, w_lo)) + b_ref[0]


def _modulation(cv, w_ada, b_ada):
    d = D_MODEL
    return pl.pallas_call(
        _mod_kernel,
        grid=(DEPTH, 6),
        in_specs=[pl.BlockSpec((MOD_ROWS, d), lambda i, j: (0, 0)),
                  pl.BlockSpec((1, d, d), lambda i, j: (i, 0, j)),
                  pl.BlockSpec((1, 1, d), lambda i, j: (i, 0, j))],
        out_specs=pl.BlockSpec((1, MOD_ROWS, d), lambda i, j: (i, 0, j)),
        out_shape=jax.ShapeDtypeStruct((DEPTH, MOD_ROWS, 6 * d), F32),
        compiler_params=_params(2),
        name="adaln_modulation",
    )(cv, w_ada, b_ada.reshape(DEPTH, 1, 6 * d))


def _columns(row):
    return jnp.broadcast_to(row, (LANES, row.shape[1])).T


def _cache_mod_columns(cols_ref, mod_refs, mod_row):
    d = D_MODEL

    @pl.when(pl.program_id(1) == 0)
    def _():
        r = pl.program_id(0) if mod_row is None else mod_row
        for i, mod_ref in enumerate(mod_refs):
            cols = _columns(mod_ref[0, pl.ds(r, 1), :])
            cols_ref[2 * i] = cols[0:d]
            cols_ref[2 * i + 1] = cols[d:2 * d]


def _wide(cols, t):
    return cols if t == LANES else jnp.concatenate([cols] * (t // LANES), axis=1)


def _modulate(x, cols_ref):
    t = x.shape[1]
    return x * (1.0 + _wide(cols_ref[1], t)) + _wide(cols_ref[0], t)


def _rms_rows(x, gain_col):
    ms = jnp.mean(x * x, axis=0, keepdims=True)
    return x * lax.rsqrt(ms + NORM_EPS) * gain_col


def _layer_norm_rows(z, g_col, b_col):
    mu = jnp.mean(z, axis=0, keepdims=True)
    zc = z - mu
    var = jnp.mean(zc * zc, axis=0, keepdims=True)
    return zc * lax.rsqrt(var + NORM_EPS) * g_col + b_col


def _rotate_rows(x, rope_ref):
    q = x.shape[0] // 4
    cr, sr, cc, sc = rope_ref[0], rope_ref[1], rope_ref[2], rope_ref[3]
    a1, a2, b1, b2 = x[0:q], x[q:2 * q], x[2 * q:3 * q], x[3 * q:4 * q]
    return jnp.concatenate([a1 * cr - a2 * sr, a2 * cr + a1 * sr,
                            b1 * cc - b2 * sc, b2 * cc + b1 * sc], axis=0)


def _store_values(v_ref, v, n_heads):
    ones = jnp.ones((ONES_ROWS, v.shape[1]), BF16)
    for j in range(n_heads):
        v_ref[0, j * V_ROWS:j * V_ROWS + HEAD_V_DIM, :] = v[j * HEAD_V_DIM:(j + 1) * HEAD_V_DIM].astype(BF16)
        v_ref[0, j * V_ROWS + HEAD_V_DIM:(j + 1) * V_ROWS, :] = ones


def _gqa_proj_kernel(x_ref, mod_ref, w_ref, gq_ref, gk_ref, *rest, rotate, natural_in, mod_row):
    rest = list(rest)
    rope_ref = rest.pop(0) if rotate else None
    cols_ref = rest.pop()
    q_ref, k_ref, v_ref = rest[:3]
    _cache_mod_columns(cols_ref, [mod_ref], mod_row)
    hd = GQA_HEAD_DIM
    nq = GQA_HEADS * hd
    nk = GQA_KV_HEADS * hd
    if natural_in:
        x = x_ref[0].T
        rest[3][0] = x
    else:
        x = x_ref[0]
    h = _modulate(x, cols_ref).astype(BF16)
    gq = gq_ref[...]
    gk = gk_ref[...]
    q_scale = hd ** -0.5 * LOG2E
    heads_per_step = PROJ_ROWS // hd

    def project(row0, rows):
        return jnp.dot(w_ref[0, row0:row0 + rows, :], h, preferred_element_type=F32)

    for g in range(nq // PROJ_ROWS):
        p = project(g * PROJ_ROWS, PROJ_ROWS)
        for i in range(heads_per_step):
            qh = _rms_rows(p[i * hd:(i + 1) * hd], gq)
            if rotate:
                qh = _rotate_rows(qh, rope_ref)
            row = g * PROJ_ROWS + i * hd
            q_ref[0, row:row + hd, :] = (qh * q_scale).astype(BF16)
    p = project(nq, nk)
    ks = []
    for j in range(GQA_KV_HEADS):
        kh = _rms_rows(p[j * hd:(j + 1) * hd], gk)
        if rotate:
            kh = _rotate_rows(kh, rope_ref)
        ks.append(kh)
    kt = jnp.concatenate(ks, axis=0).T
    for j in range(GQA_KV_HEADS):
        k_ref[0, j] = kt[:, j * hd:(j + 1) * hd].astype(BF16)
    _store_values(v_ref, project(nq + nk, nk), GQA_KV_HEADS)


def _gqa_proj(x, natural_in, mod, layer, mod_row, w_t, w_idx, gq_col, gk_col, rope, tile):
    if natural_in:
        b, n, d = x.shape
        x_spec = pl.BlockSpec((1, tile, d), lambda bi, ti: (bi, ti, 0))
    else:
        b, d, n = x.shape
        x_spec = pl.BlockSpec((1, d, tile), lambda bi, ti: (bi, 0, ti))
    hd = GQA_HEAD_DIM
    nq = GQA_HEADS * hd
    rotate = rope is not None
    in_specs = [x_spec,
                pl.BlockSpec((1, MOD_ROWS, 2 * d), lambda bi, ti: (layer, 0, 0)),
                _layer_spec(w_t, w_idx),
                pl.BlockSpec(gq_col.shape, lambda bi, ti: (0, 0)),
                pl.BlockSpec(gk_col.shape, lambda bi, ti: (0, 0))]
    args = [x, mod, w_t, gq_col, gk_col]
    if rotate:
        in_specs.append(pl.BlockSpec((4, hd // 4, tile), lambda bi, ti: (0, 0, ti)))
        args.append(rope)
    out_specs = [pl.BlockSpec((1, nq, tile), lambda bi, ti: (bi, 0, ti)),
                 pl.BlockSpec((1, GQA_KV_HEADS, tile, hd), lambda bi, ti: (bi, 0, ti, 0)),
                 pl.BlockSpec((1, GQA_KV_HEADS * V_ROWS, tile), lambda bi, ti: (bi, 0, ti))]
    out_shape = [jax.ShapeDtypeStruct((b, nq, n), BF16),
                 jax.ShapeDtypeStruct((b, GQA_KV_HEADS, n, hd), BF16),
                 jax.ShapeDtypeStruct((b, GQA_KV_HEADS * V_ROWS, n), BF16)]
    if natural_in:
        out_specs.append(pl.BlockSpec((1, d, tile), lambda bi, ti: (bi, 0, ti)))
        out_shape.append(jax.ShapeDtypeStruct((b, d, n), F32))
    return pl.pallas_call(
        functools.partial(_gqa_proj_kernel, rotate=rotate, natural_in=natural_in, mod_row=mod_row),
        grid=(b, n // tile),
        in_specs=in_specs,
        out_specs=out_specs,
        out_shape=out_shape,
        scratch_shapes=[pltpu.VMEM((2, d, LANES), F32)],
        compiler_params=_params(2),
        name="gqa_proj_lat" if rotate else "gqa_proj_ctx",
    )(*args)


def _mla_proj_kernel(x_ref, mod_ref, win_ref, gq_ref, gkv_ref, wuq_ref, wukv_ref, *rest, rotate, mod_row):
    if rotate:
        rope_ref, q_ref, k_ref, v_ref, cols_ref = rest
    else:
        q_ref, k_ref, v_ref, cols_ref = rest
    _cache_mod_columns(cols_ref, [mod_ref], mod_row)
    nh, nope, rd, pad = MLA_HEADS, MLA_NOPE_DIM, MLA_ROPE_DIM, MLA_QK_PAD
    t = x_ref.shape[2]
    h = _modulate(x_ref[0], cols_ref).astype(BF16)
    p = jnp.dot(win_ref[0], h, preferred_element_type=F32)
    cq = _rms_rows(p[0:MLA_Q_RANK], gq_ref[...]).astype(BF16)
    ckv = _rms_rows(p[MLA_Q_RANK:MLA_Q_RANK + MLA_KV_RANK], gkv_ref[...]).astype(BF16)
    k_pe = p[MLA_Q_RANK + MLA_KV_RANK:MLA_Q_RANK + MLA_KV_RANK + rd]
    if rotate:
        k_pe = _rotate_rows(k_pe, rope_ref)
    q = jnp.dot(wuq_ref[0], cq, preferred_element_type=F32)
    kv = jnp.dot(wukv_ref[0], ckv, preferred_element_type=F32)
    q_scale = MLA_QK_DIM ** -0.5 * LOG2E
    zeros = jnp.zeros((pad - nope - rd, t), F32)
    qk = MLA_QK_DIM
    for i in range(nh):
        q_pe = q[i * qk + nope:(i + 1) * qk]
        if rotate:
            q_pe = _rotate_rows(q_pe, rope_ref)
        qh = jnp.concatenate([q[i * qk:i * qk + nope], q_pe], axis=0) * q_scale
        q_ref[0, i * pad:(i + 1) * pad, :] = jnp.concatenate([qh, zeros], axis=0).astype(BF16)
        kh = jnp.concatenate([kv[i * nope:(i + 1) * nope], k_pe, zeros], axis=0)
        k_ref[0, i] = kh.T.astype(BF16)
    _store_values(v_ref, kv[nh * nope:], nh)


def _mla_proj(x_t, mod, layer, mod_row, w_idx, win_t, gq_col, gkv_col, wuq_t, wukv_t, rope, tile):
    b, d, n = x_t.shape
    nh, pad = MLA_HEADS, MLA_QK_PAD
    rotate = rope is not None
    in_specs = [pl.BlockSpec((1, d, tile), lambda bi, ti: (bi, 0, ti)),
                pl.BlockSpec((1, MOD_ROWS, 2 * d), lambda bi, ti: (layer, 0, 0)),
                _layer_spec(win_t, w_idx),
                pl.BlockSpec(gq_col.shape, lambda bi, ti: (0, 0)),
                pl.BlockSpec(gkv_col.shape, lambda bi, ti: (0, 0)),
                _layer_spec(wuq_t, w_idx),
                _layer_spec(wukv_t, w_idx)]
    args = [x_t, mod, win_t, gq_col, gkv_col, wuq_t, wukv_t]
    if rotate:
        in_specs.append(pl.BlockSpec((4, MLA_ROPE_DIM // 4, tile), lambda bi, ti: (0, 0, ti)))
        args.append(rope)
    return pl.pallas_call(
        functools.partial(_mla_proj_kernel, rotate=rotate, mod_row=mod_row),
        grid=(b, n // tile),
        in_specs=in_specs,
        scratch_shapes=[pltpu.VMEM((2, d, LANES), F32)],
        out_specs=[pl.BlockSpec((1, nh * pad, tile), lambda bi, ti: (bi, 0, ti)),
                   pl.BlockSpec((1, nh, tile, pad), lambda bi, ti: (bi, 0, ti, 0)),
                   pl.BlockSpec((1, nh * V_ROWS, tile), lambda bi, ti: (bi, 0, ti))],
        out_shape=[jax.ShapeDtypeStruct((b, nh * pad, n), BF16),
                   jax.ShapeDtypeStruct((b, nh, n, pad), BF16),
                   jax.ShapeDtypeStruct((b, nh * V_ROWS, n), BF16)],
        compiler_params=_params(2),
        name="mla_proj_lat" if rotate else "mla_proj_ctx",
    )(*args)


def _attn_kernel(q_ref, *rest, n_heads, group, dk_rows, dv):
    n_src = (len(rest) - 5) // 3
    k_refs = rest[0:2 * n_src:2]
    v_refs = rest[1:2 * n_src:2]
    qn_ref = rest[2 * n_src]
    kn_refs = rest[2 * n_src + 1:3 * n_src + 1]
    o_ref = rest[3 * n_src + 1]
    s_refs = rest[3 * n_src + 2:3 * n_src + 4]
    m_ref = rest[3 * n_src + 4]
    chunks = []
    base = 0
    for src, k_ref in enumerate(k_refs):
        for lo in range(0, k_ref.shape[2], KV_CHUNK):
            chunks.append((src, lo, base + lo))
        base += k_ref.shape[2]

    def head_of_this_step(h, slot):
        kvh = h // group
        q = q_ref[0, pl.ds(pl.multiple_of(h * dk_rows, dk_rows), dk_rows), :]
        return q, (lambda src, lo: k_refs[src][0, kvh, lo:lo + KV_CHUNK, :]), slot

    def head_0_of_next_step():
        return qn_ref[0], (lambda src, lo: kn_refs[src][0, 0, lo:lo + KV_CHUNK, :]), 0

    def step(cur, nxt):
        q, load_keys, slot_n = nxt
        if cur is not None:
            h_c, slot_c, m_c = cur
            v_row = pl.multiple_of((h_c // group) * V_ROWS, ONES_ROWS)
        m_n = None
        acc = None
        for src, lo, glo in chunks:
            s = jnp.dot(load_keys(src, lo), q, preferred_element_type=F32)
            s_refs[slot_n][glo:glo + KV_CHUNK, :] = s
            ms = jnp.max(s, axis=0, keepdims=True)
            m_n = ms if m_n is None else jnp.maximum(m_n, ms)
            if cur is not None:
                p = jnp.exp2(s_refs[slot_c][glo:glo + KV_CHUNK, :] - m_c).astype(BF16)
                part = jnp.dot(v_refs[src][0, pl.ds(v_row, V_ROWS), lo:lo + KV_CHUNK], p,
                               preferred_element_type=F32)
                acc = part if acc is None else acc + part
        if cur is not None:
            out = acc[0:dv] / acc[dv:dv + 1]
            o_ref[0, pl.ds(pl.multiple_of(h_c * dv, dv), dv), :] = out.astype(BF16)
        return m_n

    @pl.when(jnp.logical_and(pl.program_id(0) == 0, pl.program_id(1) == 0))
    def _():
        m_ref[...] = step(None, head_of_this_step(0, 0))

    def run(i, m):
        h0 = i * HEADS_PER_LOOP
        for t in range(HEADS_PER_LOOP):
            m = step((h0 + t, t % 2, m), head_of_this_step(h0 + t + 1, (t + 1) % 2))
        return m

    n_loop = n_heads // HEADS_PER_LOOP - 1
    m = lax.fori_loop(0, n_loop, run, m_ref[...])
    h0 = n_loop * HEADS_PER_LOOP
    for t in range(HEADS_PER_LOOP - 1):
        m = step((h0 + t, t % 2, m), head_of_this_step(h0 + t + 1, (t + 1) % 2))
    m_ref[...] = step((n_heads - 1, 1, m), head_0_of_next_step())


def _attention(q_t, kv_sources, batch, n_heads, group, dk_rows, dv, tq, name):
    qb, qrows, nq = q_t.shape
    tiles = nq // tq
    grid = (batch, tiles)

    def next_step(bi, ti):
        flat = jnp.minimum(bi * tiles + ti + 1, batch * tiles - 1)
        return flat // tiles, flat % tiles

    q_index = lambda bi, ti: (bi, 0, ti)
    qn_index = lambda bi, ti: (next_step(bi, ti)[0], 0, next_step(bi, ti)[1])
    in_specs = [pl.BlockSpec((1, qrows, tq), q_index)]
    args = [q_t]
    next_specs = [pl.BlockSpec((1, dk_rows, tq), qn_index)]
    next_args = [q_t]
    n_keys = 0
    for k, v, keys in kv_sources:
        if k.shape[0] == 1:
            k_index, v_index = (lambda bi, ti: (0, 0, bi, 0)), (lambda bi, ti: (0, 0, bi))
            kn_index = lambda bi, ti: (0, 0, next_step(bi, ti)[0], 0)
        else:
            k_index, v_index = (lambda bi, ti: (bi, 0, 0, 0)), (lambda bi, ti: (bi, 0, 0))
            kn_index = lambda bi, ti: (next_step(bi, ti)[0], 0, 0, 0)
        in_specs.append(pl.BlockSpec((1, k.shape[1], keys, k.shape[3]), k_index))
        in_specs.append(pl.BlockSpec((1, v.shape[1], keys), v_index))
        args += [k, v]
        next_specs.append(pl.BlockSpec((1, 1, keys, k.shape[3]), kn_index))
        next_args.append(k)
        n_keys += keys
    return pl.pallas_call(
        functools.partial(_attn_kernel, n_heads=n_heads, group=group, dk_rows=dk_rows, dv=dv),
        grid=grid,
        in_specs=in_specs + next_specs,
        out_specs=pl.BlockSpec((1, n_heads * dv, tq), q_index),
        out_shape=jax.ShapeDtypeStruct((qb, n_heads * dv, nq), BF16),
        scratch_shapes=[pltpu.VMEM((n_keys, tq), F32), pltpu.VMEM((n_keys, tq), F32),
                        pltpu.VMEM((1, tq), F32)],
        compiler_params=_params(2),
        name=name,
    )(*args, *next_args)


def _ctx_attn_kernel(q_ref, k_ref, v_ref, o_ref, s_ref, *, n_heads, group, dk_rows, dv):
    maxes = []
    for h in range(n_heads):
        q = q_ref[0, h * dk_rows:(h + 1) * dk_rows, :]
        s = jnp.dot(k_ref[0, h // group], q, preferred_element_type=F32)
        s_ref[h] = s
        maxes.append(jnp.max(s, axis=0, keepdims=True))
    for h in range(n_heads):
        p = jnp.exp2(s_ref[h] - maxes[h]).astype(BF16)
        kvh = h // group
        acc = jnp.dot(v_ref[0, kvh * V_ROWS:(kvh + 1) * V_ROWS, :], p, preferred_element_type=F32)
        o_ref[0, h * dv:(h + 1) * dv, :] = (acc[0:dv] / acc[dv:dv + 1]).astype(BF16)


def _ctx_attention(q_t, k, v, batch, n_heads, group, dk_rows, dv):
    _, qrows, total = q_t.shape
    n = total // batch
    return pl.pallas_call(
        functools.partial(_ctx_attn_kernel, n_heads=n_heads, group=group, dk_rows=dk_rows, dv=dv),
        grid=(batch,),
        in_specs=[pl.BlockSpec((1, qrows, n), lambda bi: (0, 0, bi)),
                  pl.BlockSpec((1, k.shape[1], n, k.shape[3]), lambda bi: (0, 0, bi, 0)),
                  pl.BlockSpec((1, v.shape[1], n), lambda bi: (0, 0, bi))],
        out_specs=pl.BlockSpec((1, n_heads * dv, n), lambda bi: (0, 0, bi)),
        out_shape=jax.ShapeDtypeStruct((1, n_heads * dv, total), BF16),
        scratch_shapes=[pltpu.VMEM((n_heads, n, n), F32)],
        compiler_params=_params(1),
        name="attn_ctx",
    )(q_t, k, v)


def _post_kernel(x_ref, a_ref, moda_ref, modm_ref, ln_ref, wo_ref, w1_ref, w2_ref, o_ref, cols_ref, *,
                 natural_out, mod_row):
    d = D_MODEL
    t = x_ref.shape[2]
    half = t // 2
    pieces = half // LANES
    n_chunks = FFN_HIDDEN // FFN_CHUNK
    _cache_mod_columns(cols_ref, [moda_ref, modm_ref], mod_row)

    def col(i, width):
        return _wide(cols_ref[i] if i < 4 else ln_ref[0, i - 4], width)

    def merge_attn(lo, width, y):
        x1 = _layer_norm_rows(DEEPNORM_ALPHA * x_ref[0, :, lo:lo + width] + col(0, width) * y,
                              col(4, width), col(5, width))
        return x1, (x1 * (1.0 + col(2, width)) + col(1, width)).astype(BF16)

    def mlp_chunk(c, h):
        u = jnp.dot(w1_ref[0, c * FFN_CHUNK:(c + 1) * FFN_CHUNK, :], h, preferred_element_type=F32)
        u = jnp.maximum(u, 0.0)
        u = (u * u).astype(BF16)
        return jnp.dot(w2_ref[0, :, c * FFN_CHUNK:(c + 1) * FFN_CHUNK], u, preferred_element_type=F32)

    def merge_mlp(lo, x1, acc):
        x2 = _layer_norm_rows(DEEPNORM_ALPHA * x1 + col(3, LANES) * acc, col(6, LANES), col(7, LANES))
        if natural_out:
            o_ref[0, lo:lo + LANES, :] = x2.T
        else:
            o_ref[0, :, lo:lo + LANES] = x2

    y_a = jnp.dot(wo_ref[0], a_ref[0, :, 0:half], preferred_element_type=F32)
    y_b = jnp.dot(wo_ref[0], a_ref[0, :, half:t], preferred_element_type=F32)
    x1_a, h_a = merge_attn(0, half, y_a)
    acc_a = None
    x1_b, h_b = [], []
    for c in range(n_chunks):
        part = mlp_chunk(c, h_a)
        acc_a = part if acc_a is None else acc_a + part
        if c < pieces:
            x1, h = merge_attn(half + c * LANES, LANES, y_b[:, c * LANES:(c + 1) * LANES])
            x1_b.append(x1)
            h_b.append(h)
    h_b = jnp.concatenate(h_b, axis=1)
    acc_b = None
    for c in range(n_chunks):
        part = mlp_chunk(c, h_b)
        acc_b = part if acc_b is None else acc_b + part
        if c < pieces:
            merge_mlp(c * LANES, x1_a[:, c * LANES:(c + 1) * LANES], acc_a[:, c * LANES:(c + 1) * LANES])
    for j in range(pieces):
        merge_mlp(half + j * LANES, x1_b[j], acc_b[:, j * LANES:(j + 1) * LANES])


def _post(x_t, attn_t, mod, ln_cols, layer, mod_row, wo_t, wo_idx, w1_t, w2_t, tile, natural_out, name):
    b, d, n = x_t.shape
    single = dict(pipeline_mode=pl.Buffered(1))
    if natural_out:
        out_spec = pl.BlockSpec((1, tile, d), lambda bi, ti: (bi, ti, 0))
        out_shape = jax.ShapeDtypeStruct((b, n, d), F32)
    else:
        out_spec = pl.BlockSpec((1, d, tile), lambda bi, ti: (bi, 0, ti))
        out_shape = jax.ShapeDtypeStruct((b, d, n), F32)
    return pl.pallas_call(
        functools.partial(_post_kernel, natural_out=natural_out, mod_row=mod_row),
        grid=(b, n // tile),
        in_specs=[pl.BlockSpec((1, d, tile), lambda bi, ti: (bi, 0, ti)),
                  pl.BlockSpec((1, d, tile), lambda bi, ti: (bi, 0, ti)),
                  pl.BlockSpec((1, MOD_ROWS, 2 * d), lambda bi, ti: (layer, 0, 1)),
                  pl.BlockSpec((1, MOD_ROWS, 2 * d), lambda bi, ti: (layer, 0, 2)),
                  pl.BlockSpec((1, 4, d, LANES), lambda bi, ti: (layer, 0, 0, 0), **single),
                  _layer_spec(wo_t, wo_idx, **single),
                  _layer_spec(w1_t, layer, **single),
                  _layer_spec(w2_t, layer, **single)],
        out_specs=out_spec,
        out_shape=out_shape,
        scratch_shapes=[pltpu.VMEM((4, d, LANES), F32)],
        compiler_params=_params(2),
        name=name,
    )(x_t, attn_t, mod, mod, ln_cols, wo_t, w1_t, w2_t)


def _rope_tables_t(n_tok, rot_dim):
    rows = n_tok // GRID_W
    row = jnp.broadcast_to(jnp.arange(rows, dtype=F32)[:, None], (rows, GRID_W)).reshape(-1)
    col = jnp.broadcast_to(jnp.arange(GRID_W, dtype=F32)[None, :], (rows, GRID_W)).reshape(-1)
    axis_dim = rot_dim // 2
    inv_freq = ROPE_THETA ** (-jnp.arange(0, axis_dim, 2, dtype=F32) / axis_dim)
    ang_row = inv_freq[:, None] * row[None, :]
    ang_col = inv_freq[:, None] * col[None, :]
    return jnp.stack([jnp.cos(ang_row), jnp.sin(ang_row), jnp.cos(ang_col), jnp.sin(ang_col)])


def _col(v):
    return v.astype(F32)[:, None]


def kernel(x, c, ctx, c_ctx, w_ada, b_ada, ln_g, ln_b, mlp_w1, mlp_w2, gqa_w_qkv, gqa_q_norm, gqa_k_norm, gqa_w_o, mla_w_in, mla_q_norm, mla_kv_norm, mla_w_uq, mla_w_ukv, mla_w_o):
    b, n_lat, d = x.shape
    n_ctx = ctx.shape[1]

    cv = jnp.concatenate([c, c_ctx[None, :], jnp.zeros((MOD_ROWS - b - 1, d), F32)], axis=0)
    mod = _modulation(cv, w_ada, b_ada)
    ln_cols = jnp.stack([ln_g[:, 0], ln_b[:, 0], ln_g[:, 1], ln_b[:, 1]], axis=1)
    ln_cols = jnp.broadcast_to(ln_cols[..., None], ln_cols.shape + (LANES,))

    lat_mod = None
    ctx_mod = b

    rope_gqa = _rope_tables_t(n_lat, GQA_HEAD_DIM)
    rope_mla = _rope_tables_t(n_lat, MLA_ROPE_DIM)

    w1_t, w2_t = _transpose_cast(mlp_w1), _transpose_cast(mlp_w2)
    gqa_w_t, gqa_wo_t = _transpose_cast(gqa_w_qkv), _transpose_cast(gqa_w_o)
    mla_wo_t = _transpose_cast(mla_w_o)
    n_b = mla_w_in.shape[0]
    mla_win_t = mla_w_in.swapaxes(1, 2).astype(BF16)
    mla_wuq_t = mla_w_uq.swapaxes(1, 2).astype(BF16)
    wukv = mla_w_ukv.reshape(n_b, MLA_KV_RANK, MLA_HEADS, MLA_NOPE_DIM + MLA_V_DIM)
    wukv = jnp.concatenate([wukv[..., :MLA_NOPE_DIM].reshape(n_b, MLA_KV_RANK, -1),
                            wukv[..., MLA_NOPE_DIM:].reshape(n_b, MLA_KV_RANK, -1)], axis=2)
    mla_wukv_t = wukv.swapaxes(1, 2).astype(BF16)

    xc_nat = ctx.reshape(1, b * n_ctx, d)
    x_t = xc_t = None

    for i in range(DEPTH):
        need_ctx = i < DEPTH - 1
        last = i == DEPTH - 1
        j = i // N_MIXERS
        if i % N_MIXERS == 0:
            wo_t = gqa_wo_t
            gq, gk = _col(gqa_q_norm[j]), _col(gqa_k_norm[j])
            if i == 0:
                q_l, k_l, v_l, x_t = _gqa_proj(x, True, mod, i, lat_mod, gqa_w_t, j, gq, gk, rope_gqa, LAT_TILE)
                q_c, k_c, v_c, xc_t = _gqa_proj(xc_nat, True, mod, i, ctx_mod, gqa_w_t, j, gq, gk, None, LAT_TILE)
            else:
                q_l, k_l, v_l = _gqa_proj(x_t, False, mod, i, lat_mod, gqa_w_t, j, gq, gk, rope_gqa, LAT_TILE)
                q_c, k_c, v_c = _gqa_proj(xc_t, False, mod, i, ctx_mod, gqa_w_t, j, gq, gk, None, LAT_TILE)
            heads, group, dk_rows, dv = GQA_HEADS, GQA_HEADS // GQA_KV_HEADS, GQA_HEAD_DIM, GQA_HEAD_DIM
        else:
            wo_t = mla_wo_t
            gq, gkv = _col(mla_q_norm[j]), _col(mla_kv_norm[j])
            q_l, k_l, v_l = _mla_proj(x_t, mod, i, lat_mod, j, mla_win_t, gq, gkv, mla_wuq_t, mla_wukv_t,
                                      rope_mla, LAT_TILE)
            q_c, k_c, v_c = _mla_proj(xc_t, mod, i, ctx_mod, j, mla_win_t, gq, gkv, mla_wuq_t, mla_wukv_t,
                                      None, LAT_TILE)
            heads, group, dk_rows, dv = MLA_HEADS, 1, MLA_QK_PAD, MLA_V_DIM

        a_l = _attention(q_l, [(k_c, v_c, n_ctx), (k_l, v_l, n_lat)], b, heads, group, dk_rows, dv, LAT_TILE,
                         "attn_lat")
        x_t = _post(x_t, a_l, mod, ln_cols, i, lat_mod, wo_t, j, w1_t, w2_t, POST_TILE, last, "post_lat")
        if need_ctx:
            a_c = _ctx_attention(q_c, k_c, v_c, b, heads, group, dk_rows, dv)
            xc_t = _post(xc_t, a_c, mod, ln_cols, i, ctx_mod, wo_t, j, w1_t, w2_t, POST_TILE, False, "post_ctx")

    return x_t
```

```python
import functools

import jax
import jax.numpy as jnp
from jax import lax
from jax.experimental import pallas as pl
from jax.experimental.pallas import tpu as pltpu

D_MODEL = 1024
DEPTH = 4
GRID_W = 64
N_MIXERS = 2
GQA_HEADS = 16
GQA_KV_HEADS = 4
GQA_HEAD_DIM = 64
MLA_HEADS = 16
MLA_Q_RANK = 384
MLA_KV_RANK = 256
MLA_NOPE_DIM = 64
MLA_ROPE_DIM = 32
MLA_V_DIM = 64
MLA_QK_DIM = MLA_NOPE_DIM + MLA_ROPE_DIM
MLA_QK_PAD = 128
FFN_HIDDEN = 4 * D_MODEL
ROPE_THETA = 10000.0
NORM_EPS = 1e-6
DEEPNORM_ALPHA = (2.0 * DEPTH) ** 0.25
LOG2E = 1.4426950408889634

HEAD_V_DIM = 64
ONES_ROWS = 16
V_ROWS = HEAD_V_DIM + ONES_ROWS

LAT_TILE = 512
POST_TILE = 1024
KV_CHUNK = 256
HEADS_PER_LOOP = 8
FFN_CHUNK = 1024
WEIGHT_TILE = 1024
MOD_ROWS = 16
PROJ_ROWS = 256
LANES = 128
VMEM_LIMIT_BYTES = 60 * 1024 * 1024

F32 = jnp.float32
BF16 = jnp.bfloat16


def _params(n_axes):
    return pltpu.CompilerParams(dimension_semantics=("arbitrary",) * n_axes,
                                vmem_limit_bytes=VMEM_LIMIT_BYTES)


def _transpose_cast_kernel(w_ref, o_ref):
    o_ref[0] = w_ref[0].T.astype(BF16)


def _transpose_cast(w):
    n_l, k, m = w.shape
    tk = WEIGHT_TILE if k % WEIGHT_TILE == 0 else WEIGHT_TILE // 2
    tm = WEIGHT_TILE if m % WEIGHT_TILE == 0 else WEIGHT_TILE // 2
    return pl.pallas_call(
        _transpose_cast_kernel,
        grid=(n_l, k // tk, m // tm),
        in_specs=[pl.BlockSpec((1, tk, tm), lambda l, i, j: (l, i, j))],
        out_specs=pl.BlockSpec((1, tm, tk), lambda l, i, j: (l, j, i)),
        out_shape=jax.ShapeDtypeStruct((n_l, m, k), BF16),
        compiler_params=_params(3),
        name="weight_transpose_cast",
    )(w)


def _layer_spec(w, idx, **kw):
    return pl.BlockSpec((1,) + w.shape[1:], lambda bi, ti: (idx, 0, 0), **kw)


def _mod_kernel(cv_ref, w_ref, b_ref, o_ref):
    c = cv_ref[...]
    s = c * jax.nn.sigmoid(c)
    w = w_ref[0]
    s_hi = s.astype(BF16)
    s_lo = (s - s_hi.astype(F32)).astype(BF16)
    w_hi = w.astype(BF16)
    w_lo = (w - w_hi.astype(F32)).astype(BF16)
    dot = functools.partial(jnp.dot, preferred_element_type=F32)
    o_ref[0] = dot(s_hi, w_hi) + (dot(s_lo, w_hi) + dot(s_hi, w_lo)) + b_ref[0]


def _modulation(cv, w_ada, b_ada):
    d = D_MODEL
    return pl.pallas_call(
        _mod_kernel,
        grid=(DEPTH, 6),
        in_specs=[pl.BlockSpec((MOD_ROWS, d), lambda i, j: (0, 0)),
                  pl.BlockSpec((1, d, d), lambda i, j: (i, 0, j)),
                  pl.BlockSpec((1, 1, d), lambda i, j: (i, 0, j))],
        out_specs=pl.BlockSpec((1, MOD_ROWS, d), lambda i, j: (i, 0, j)),
        out_shape=jax.ShapeDtypeStruct((DEPTH, MOD_ROWS, 6 * d), F32),
        compiler_params=_params(2),
        name="adaln_modulation",
    )(cv, w_ada, b_ada.reshape(DEPTH, 1, 6 * d))


def _columns(row):
    return jnp.broadcast_to(row, (LANES, row.shape[1])).T


def _cache_mod_columns(cols_ref, mod_refs, mod_row):
    d = D_MODEL

    @pl.when(pl.program_id(1) == 0)
    def _():
        r = pl.program_id(0) if mod_row is None else mod_row
        for i, mod_ref in enumerate(mod_refs):
            cols = _columns(mod_ref[0, pl.ds(r, 1), :])
            cols_ref[2 * i] = cols[0:d]
            cols_ref[2 * i + 1] = cols[d:2 * d]


def _wide(cols, t):
    return cols if t == LANES else jnp.concatenate([cols] * (t // LANES), axis=1)


def _modulate(x, cols_ref):
    t = x.shape[1]
    return x * (1.0 + _wide(cols_ref[1], t)) + _wide(cols_ref[0], t)


def _rms_rows(x, gain_col):
    ms = jnp.mean(x * x, axis=0, keepdims=True)
    return x * lax.rsqrt(ms + NORM_EPS) * gain_col


def _layer_norm_rows(z, g_col, b_col):
    mu = jnp.mean(z, axis=0, keepdims=True)
    zc = z - mu
    var = jnp.mean(zc * zc, axis=0, keepdims=True)
    return zc * lax.rsqrt(var + NORM_EPS) * g_col + b_col


def _rotate_rows(x, rope_ref):
    q = x.shape[0] // 4
    cr, sr, cc, sc = rope_ref[0], rope_ref[1], rope_ref[2], rope_ref[3]
    a1, a2, b1, b2 = x[0:q], x[q:2 * q], x[2 * q:3 * q], x[3 * q:4 * q]
    return jnp.concatenate([a1 * cr - a2 * sr, a2 * cr + a1 * sr,
                            b1 * cc - b2 * sc, b2 * cc + b1 * sc], axis=0)


def _store_values(v_ref, v, n_heads):
    ones = jnp.ones((ONES_ROWS, v.shape[1]), BF16)
    for j in range(n_heads):
        v_ref[0, j * V_ROWS:j * V_ROWS + HEAD_V_DIM, :] = v[j * HEAD_V_DIM:(j + 1) * HEAD_V_DIM].astype(BF16)
        v_ref[0, j * V_ROWS + HEAD_V_DIM:(j + 1) * V_ROWS, :] = ones


def _gqa_proj_kernel(x_ref, mod_ref, w_ref, gq_ref, gk_ref, *rest, rotate, natural_in, mod_row):
    rest = list(rest)
    rope_ref = rest.pop(0) if rotate else None
    q_ref, k_ref, v_ref, cols_ref = rest
    _cache_mod_columns(cols_ref, [mod_ref], mod_row)
    hd = GQA_HEAD_DIM
    nq = GQA_HEADS * hd
    nk = GQA_KV_HEADS * hd
    x = x_ref[0].T if natural_in else x_ref[0]
    h = _modulate(x, cols_ref).astype(BF16)
    gq = gq_ref[...]
    gk = gk_ref[...]
    q_scale = hd ** -0.5 * LOG2E
    heads_per_step = PROJ_ROWS // hd

    def project(row0, rows):
        return jnp.dot(w_ref[0, row0:row0 + rows, :], h, preferred_element_type=F32)

    for g in range(nq // PROJ_ROWS):
        p = project(g * PROJ_ROWS, PROJ_ROWS)
        for i in range(heads_per_step):
            qh = _rms_rows(p[i * hd:(i + 1) * hd], gq)
            if rotate:
                qh = _rotate_rows(qh, rope_ref)
            row = g * PROJ_ROWS + i * hd
            q_ref[0, row:row + hd, :] = (qh * q_scale).astype(BF16)
    p = project(nq, nk)
    ks = []
    for j in range(GQA_KV_HEADS):
        kh = _rms_rows(p[j * hd:(j + 1) * hd], gk)
        if rotate:
            kh = _rotate_rows(kh, rope_ref)
        ks.append(kh)
    kt = jnp.concatenate(ks, axis=0).T
    for j in range(GQA_KV_HEADS):
        k_ref[0, j] = kt[:, j * hd:(j + 1) * hd].astype(BF16)
    _store_values(v_ref, project(nq + nk, nk), GQA_KV_HEADS)


def _gqa_proj(x, natural_in, mod, layer, mod_row, w_t, w_idx, gq_col, gk_col, rope, tile):
    if natural_in:
        b, n, d = x.shape
        x_spec = pl.BlockSpec((1, tile, d), lambda bi, ti: (bi, ti, 0))
    else:
        b, d, n = x.shape
        x_spec = pl.BlockSpec((1, d, tile), lambda bi, ti: (bi, 0, ti))
    hd = GQA_HEAD_DIM
    nq = GQA_HEADS * hd
    rotate = rope is not None
    in_specs = [x_spec,
                pl.BlockSpec((1, MOD_ROWS, 2 * d), lambda bi, ti: (layer, 0, 0)),
                _layer_spec(w_t, w_idx),
                pl.BlockSpec(gq_col.shape, lambda bi, ti: (0, 0)),
                pl.BlockSpec(gk_col.shape, lambda bi, ti: (0, 0))]
    args = [x, mod, w_t, gq_col, gk_col]
    if rotate:
        in_specs.append(pl.BlockSpec((4, hd // 4, tile), lambda bi, ti: (0, 0, ti)))
        args.append(rope)
    out_specs = [pl.BlockSpec((1, nq, tile), lambda bi, ti: (bi, 0, ti)),
                 pl.BlockSpec((1, GQA_KV_HEADS, tile, hd), lambda bi, ti: (bi, 0, ti, 0)),
                 pl.BlockSpec((1, GQA_KV_HEADS * V_ROWS, tile), lambda bi, ti: (bi, 0, ti))]
    out_shape = [jax.ShapeDtypeStruct((b, nq, n), BF16),
                 jax.ShapeDtypeStruct((b, GQA_KV_HEADS, n, hd), BF16),
                 jax.ShapeDtypeStruct((b, GQA_KV_HEADS * V_ROWS, n), BF16)]
    return pl.pallas_call(
        functools.partial(_gqa_proj_kernel, rotate=rotate, natural_in=natural_in, mod_row=mod_row),
        grid=(b, n // tile),
        in_specs=in_specs,
        out_specs=out_specs,
        out_shape=out_shape,
        scratch_shapes=[pltpu.VMEM((2, d, LANES), F32)],
        compiler_params=_params(2),
        name="gqa_proj_lat" if rotate else "gqa_proj_ctx",
    )(*args)


def _mla_proj_kernel(x_ref, mod_ref, win_ref, gq_ref, gkv_ref, wuq_ref, wukv_ref, *rest, rotate, mod_row):
    if rotate:
        rope_ref, q_ref, k_ref, v_ref, cols_ref = rest
    else:
        q_ref, k_ref, v_ref, cols_ref = rest
    _cache_mod_columns(cols_ref, [mod_ref], mod_row)
    nh, nope, rd, pad = MLA_HEADS, MLA_NOPE_DIM, MLA_ROPE_DIM, MLA_QK_PAD
    t = x_ref.shape[2]
    h = _modulate(x_ref[0], cols_ref).astype(BF16)
    p = jnp.dot(win_ref[0], h, preferred_element_type=F32)
    cq = _rms_rows(p[0:MLA_Q_RANK], gq_ref[...]).astype(BF16)
    ckv = _rms_rows(p[MLA_Q_RANK:MLA_Q_RANK + MLA_KV_RANK], gkv_ref[...]).astype(BF16)
    k_pe = p[MLA_Q_RANK + MLA_KV_RANK:MLA_Q_RANK + MLA_KV_RANK + rd]
    if rotate:
        k_pe = _rotate_rows(k_pe, rope_ref)
    q = jnp.dot(wuq_ref[0], cq, preferred_element_type=F32)
    kv = jnp.dot(wukv_ref[0], ckv, preferred_element_type=F32)
    q_scale = MLA_QK_DIM ** -0.5 * LOG2E
    zeros = jnp.zeros((pad - nope - rd, t), F32)
    qk = MLA_QK_DIM
    for i in range(nh):
        q_pe = q[i * qk + nope:(i + 1) * qk]
        if rotate:
            q_pe = _rotate_rows(q_pe, rope_ref)
        qh = jnp.concatenate([q[i * qk:i * qk + nope], q_pe], axis=0) * q_scale
        q_ref[0, i * pad:(i + 1) * pad, :] = jnp.concatenate([qh, zeros], axis=0).astype(BF16)
        kh = jnp.concatenate([kv[i * nope:(i + 1) * nope], k_pe, zeros], axis=0)
        k_ref[0, i] = kh.T.astype(BF16)
    _store_values(v_ref, kv[nh * nope:], nh)


def _mla_proj(x_t, mod, layer, mod_row, w_idx, win_t, gq_col, gkv_col, wuq_t, wukv_t, rope, tile):
    b, d, n = x_t.shape
    nh, pad = MLA_HEADS, MLA_QK_PAD
    rotate = rope is not None
    in_specs = [pl.BlockSpec((1, d, tile), lambda bi, ti: (bi, 0, ti)),
                pl.BlockSpec((1, MOD_ROWS, 2 * d), lambda bi, ti: (layer, 0, 0)),
                _layer_spec(win_t, w_idx),
                pl.BlockSpec(gq_col.shape, lambda bi, ti: (0, 0)),
                pl.BlockSpec(gkv_col.shape, lambda bi, ti: (0, 0)),
                _layer_spec(wuq_t, w_idx),
                _layer_spec(wukv_t, w_idx)]
    args = [x_t, mod, win_t, gq_col, gkv_col, wuq_t, wukv_t]
    if rotate:
        in_specs.append(pl.BlockSpec((4, MLA_ROPE_DIM // 4, tile), lambda bi, ti: (0, 0, ti)))
        args.append(rope)
    return pl.pallas_call(
        functools.partial(_mla_proj_kernel, rotate=rotate, mod_row=mod_row),
        grid=(b, n // tile),
        in_specs=in_specs,
        scratch_shapes=[pltpu.VMEM((2, d, LANES), F32)],
        out_specs=[pl.BlockSpec((1, nh * pad, tile), lambda bi, ti: (bi, 0, ti)),
                   pl.BlockSpec((1, nh, tile, pad), lambda bi, ti: (bi, 0, ti, 0)),
                   pl.BlockSpec((1, nh * V_ROWS, tile), lambda bi, ti: (bi, 0, ti))],
        out_shape=[jax.ShapeDtypeStruct((b, nh * pad, n), BF16),
                   jax.ShapeDtypeStruct((b, nh, n, pad), BF16),
                   jax.ShapeDtypeStruct((b, nh * V_ROWS, n), BF16)],
        compiler_params=_params(2),
        name="mla_proj_lat" if rotate else "mla_proj_ctx",
    )(*args)


def _attn_kernel(q_ref, *rest, n_heads, group, dk_rows, dv):
    n_src = (len(rest) - 5) // 3
    k_refs = rest[0:2 * n_src:2]
    v_refs = rest[1:2 * n_src:2]
    qn_ref = rest[2 * n_src]
    kn_refs = rest[2 * n_src + 1:3 * n_src + 1]
    o_ref = rest[3 * n_src + 1]
    s_refs = rest[3 * n_src + 2:3 * n_src + 4]
    m_ref = rest[3 * n_src + 4]
    chunks = []
    base = 0
    for src, k_ref in enumerate(k_refs):
        for lo in range(0, k_ref.shape[2], KV_CHUNK):
            chunks.append((src, lo, base + lo))
        base += k_ref.shape[2]

    def head_of_this_step(h, slot):
        kvh = h // group
        q = q_ref[0, pl.ds(pl.multiple_of(h * dk_rows, dk_rows), dk_rows), :]
        return q, (lambda src, lo: k_refs[src][0, kvh, lo:lo + KV_CHUNK, :]), slot

    def head_0_of_next_step():
        return qn_ref[0], (lambda src, lo: kn_refs[src][0, 0, lo:lo + KV_CHUNK, :]), 0

    def step(cur, nxt):
        q, load_keys, slot_n = nxt
        if cur is not None:
            h_c, slot_c, m_c = cur
            v_row = pl.multiple_of((h_c // group) * V_ROWS, ONES_ROWS)
        m_n = None
        acc = None
        for src, lo, glo in chunks:
            s = jnp.dot(load_keys(src, lo), q, preferred_element_type=F32)
            s_refs[slot_n][glo:glo + KV_CHUNK, :] = s
            ms = jnp.max(s, axis=0, keepdims=True)
            m_n = ms if m_n is None else jnp.maximum(m_n, ms)
            if cur is not None:
                p = jnp.exp2(s_refs[slot_c][glo:glo + KV_CHUNK, :] - m_c).astype(BF16)
                part = jnp.dot(v_refs[src][0, pl.ds(v_row, V_ROWS), lo:lo + KV_CHUNK], p,
                               preferred_element_type=F32)
                acc = part if acc is None else acc + part
        if cur is not None:
            out = acc[0:dv] / acc[dv:dv + 1]
            o_ref[0, pl.ds(pl.multiple_of(h_c * dv, dv), dv), :] = out.astype(BF16)
        return m_n

    @pl.when(jnp.logical_and(pl.program_id(0) == 0, pl.program_id(1) == 0))
    def _():
        m_ref[...] = step(None, head_of_this_step(0, 0))

    def run(i, m):
        h0 = i * HEADS_PER_LOOP
        for t in range(HEADS_PER_LOOP):
            m = step((h0 + t, t % 2, m), head_of_this_step(h0 + t + 1, (t + 1) % 2))
        return m

    n_loop = n_heads // HEADS_PER_LOOP - 1
    m = lax.fori_loop(0, jnp.minimum(pl.program_id(0) + n_loop, n_loop), run, m_ref[...])
    h0 = n_loop * HEADS_PER_LOOP
    for t in range(HEADS_PER_LOOP - 1):
        m = step((h0 + t, t % 2, m), head_of_this_step(h0 + t + 1, (t + 1) % 2))
    m_ref[...] = step((n_heads - 1, 1, m), head_0_of_next_step())


def _attention(q_t, kv_sources, batch, n_heads, group, dk_rows, dv, tq, name):
    qb, qrows, nq = q_t.shape
    tiles = nq // tq
    grid = (batch, tiles)

    def next_step(bi, ti):
        flat = jnp.minimum(bi * tiles + ti + 1, batch * tiles - 1)
        return flat // tiles, flat % tiles

    q_index = lambda bi, ti: (bi, 0, ti)
    qn_index = lambda bi, ti: (next_step(bi, ti)[0], 0, next_step(bi, ti)[1])
    in_specs = [pl.BlockSpec((1, qrows, tq), q_index)]
    args = [q_t]
    next_specs = [pl.BlockSpec((1, dk_rows, tq), qn_index)]
    next_args = [q_t]
    n_keys = 0
    for k, v, keys in kv_sources:
        if k.shape[0] == 1:
            k_index, v_index = (lambda bi, ti: (0, 0, bi, 0)), (lambda bi, ti: (0, 0, bi))
            kn_index = lambda bi, ti: (0, 0, next_step(bi, ti)[0], 0)
        else:
            k_index, v_index = (lambda bi, ti: (bi, 0, 0, 0)), (lambda bi, ti: (bi, 0, 0))
            kn_index = lambda bi, ti: (next_step(bi, ti)[0], 0, 0, 0)
        in_specs.append(pl.BlockSpec((1, k.shape[1], keys, k.shape[3]), k_index))
        in_specs.append(pl.BlockSpec((1, v.shape[1], keys), v_index))
        args += [k, v]
        next_specs.append(pl.BlockSpec((1, 1, keys, k.shape[3]), kn_index))
        next_args.append(k)
        n_keys += keys
    return pl.pallas_call(
        functools.partial(_attn_kernel, n_heads=n_heads, group=group, dk_rows=dk_rows, dv=dv),
        grid=grid,
        in_specs=in_specs + next_specs,
        out_specs=pl.BlockSpec((1, n_heads * dv, tq), q_index),
        out_shape=jax.ShapeDtypeStruct((qb, n_heads * dv, nq), BF16),
        scratch_shapes=[pltpu.VMEM((n_keys, tq), F32), pltpu.VMEM((n_keys, tq), F32),
                        pltpu.VMEM((1, tq), F32)],
        compiler_params=_params(2),
        name=name,
    )(*args, *next_args)


def _ctx_attn_kernel(q_ref, k_ref, v_ref, o_ref, s_ref, *, n_heads, group, dk_rows, dv):
    maxes = []
    for h in range(n_heads):
        q = q_ref[0, h * dk_rows:(h + 1) * dk_rows, :]
        s = jnp.dot(k_ref[0, h // group], q, preferred_element_type=F32)
        s_ref[h] = s
        maxes.append(jnp.max(s, axis=0, keepdims=True))
    for h in range(n_heads):
        p = jnp.exp2(s_ref[h] - maxes[h]).astype(BF16)
        kvh = h // group
        acc = jnp.dot(v_ref[0, kvh * V_ROWS:(kvh + 1) * V_ROWS, :], p, preferred_element_type=F32)
        o_ref[0, h * dv:(h + 1) * dv, :] = (acc[0:dv] / acc[dv:dv + 1]).astype(BF16)


def _ctx_attention(q_t, k, v, batch, n_heads, group, dk_rows, dv):
    _, qrows, total = q_t.shape
    n = total // batch
    return pl.pallas_call(
        functools.partial(_ctx_attn_kernel, n_heads=n_heads, group=group, dk_rows=dk_rows, dv=dv),
        grid=(batch,),
        in_specs=[pl.BlockSpec((1, qrows, n), lambda bi: (0, 0, bi)),
                  pl.BlockSpec((1, k.shape[1], n, k.shape[3]), lambda bi: (0, 0, bi, 0)),
                  pl.BlockSpec((1, v.shape[1], n), lambda bi: (0, 0, bi))],
        out_specs=pl.BlockSpec((1, n_heads * dv, n), lambda bi: (0, 0, bi)),
        out_shape=jax.ShapeDtypeStruct((1, n_heads * dv, total), BF16),
        scratch_shapes=[pltpu.VMEM((n_heads, n, n), F32)],
        compiler_params=_params(1),
        name="attn_ctx",
    )(q_t, k, v)


def _post_kernel(x_ref, a_ref, moda_ref, modm_ref, ln_ref, wo_ref, w1_ref, w2_ref, o_ref, cols_ref, *,
                 natural_in, natural_out, mod_row):
    d = D_MODEL
    t = a_ref.shape[2]
    half = t // 2
    pieces = half // LANES
    n_chunks = FFN_HIDDEN // FFN_CHUNK
    _cache_mod_columns(cols_ref, [moda_ref, modm_ref], mod_row)

    def col(i, width):
        return _wide(cols_ref[i] if i < 4 else ln_ref[0, i - 4], width)

    def merge_attn(lo, width, y):
        x = x_ref[0, lo:lo + width, :].T if natural_in else x_ref[0, :, lo:lo + width]
        x1 = _layer_norm_rows(DEEPNORM_ALPHA * x + col(0, width) * y, col(4, width), col(5, width))
        return x1, (x1 * (1.0 + col(2, width)) + col(1, width)).astype(BF16)

    def mlp_chunk(c, h):
        u = jnp.dot(w1_ref[0, c * FFN_CHUNK:(c + 1) * FFN_CHUNK, :], h, preferred_element_type=F32)
        u = jnp.maximum(u, 0.0)
        u = (u * u).astype(BF16)
        return jnp.dot(w2_ref[0, :, c * FFN_CHUNK:(c + 1) * FFN_CHUNK], u, preferred_element_type=F32)

    def merge_mlp(lo, x1, acc):
        x2 = _layer_norm_rows(DEEPNORM_ALPHA * x1 + col(3, LANES) * acc, col(6, LANES), col(7, LANES))
        if natural_out:
            o_ref[0, lo:lo + LANES, :] = x2.T
        else:
            o_ref[0, :, lo:lo + LANES] = x2

    y_a = jnp.dot(wo_ref[0], a_ref[0, :, 0:half], preferred_element_type=F32)
    y_b = jnp.dot(wo_ref[0], a_ref[0, :, half:t], preferred_element_type=F32)
    x1_a, h_a = merge_attn(0, half, y_a)
    acc_a = None
    x1_b, h_b = [], []
    for c in range(n_chunks):
        part = mlp_chunk(c, h_a)
        acc_a = part if acc_a is None else acc_a + part
        if c < pieces:
            x1, h = merge_attn(half + c * LANES, LANES, y_b[:, c * LANES:(c + 1) * LANES])
            x1_b.append(x1)
            h_b.append(h)
    h_b = jnp.concatenate(h_b, axis=1)
    acc_b = None
    for c in range(n_chunks):
        part = mlp_chunk(c, h_b)
        acc_b = part if acc_b is None else acc_b + part
        if c < pieces:
            merge_mlp(c * LANES, x1_a[:, c * LANES:(c + 1) * LANES], acc_a[:, c * LANES:(c + 1) * LANES])
    for j in range(pieces):
        merge_mlp(half + j * LANES, x1_b[j], acc_b[:, j * LANES:(j + 1) * LANES])


def _post(x, natural_in, attn_t, mod, ln_cols, layer, mod_row, wo_t, wo_idx, w1_t, w2_t, tile, natural_out,
          name):
    b, d, n = attn_t.shape
    if natural_in:
        x_spec = pl.BlockSpec((1, tile, d), lambda bi, ti: (bi, ti, 0))
    else:
        x_spec = pl.BlockSpec((1, d, tile), lambda bi, ti: (bi, 0, ti))
    single = dict(pipeline_mode=pl.Buffered(1))
    if natural_out:
        out_spec = pl.BlockSpec((1, tile, d), lambda bi, ti: (bi, ti, 0))
        out_shape = jax.ShapeDtypeStruct((b, n, d), F32)
    else:
        out_spec = pl.BlockSpec((1, d, tile), lambda bi, ti: (bi, 0, ti))
        out_shape = jax.ShapeDtypeStruct((b, d, n), F32)
    return pl.pallas_call(
        functools.partial(_post_kernel, natural_in=natural_in, natural_out=natural_out, mod_row=mod_row),
        grid=(b, n // tile),
        in_specs=[x_spec,
                  pl.BlockSpec((1, d, tile), lambda bi, ti: (bi, 0, ti)),
                  pl.BlockSpec((1, MOD_ROWS, 2 * d), lambda bi, ti: (layer, 0, 1)),
                  pl.BlockSpec((1, MOD_ROWS, 2 * d), lambda bi, ti: (layer, 0, 2)),
                  pl.BlockSpec((1, 4, d, LANES), lambda bi, ti: (layer, 0, 0, 0), **single),
                  _layer_spec(wo_t, wo_idx, **single),
                  _layer_spec(w1_t, layer, **single),
                  _layer_spec(w2_t, layer, **single)],
        out_specs=out_spec,
        out_shape=out_shape,
        scratch_shapes=[pltpu.VMEM((4, d, LANES), F32)],
        compiler_params=_params(2),
        name=name,
    )(x, attn_t, mod, mod, ln_cols, wo_t, w1_t, w2_t)


def _rope_tables_t(n_tok, rot_dim):
    rows = n_tok // GRID_W
    row = jnp.broadcast_to(jnp.arange(rows, dtype=F32)[:, None], (rows, GRID_W)).reshape(-1)
    col = jnp.broadcast_to(jnp.arange(GRID_W, dtype=F32)[None, :], (rows, GRID_W)).reshape(-1)
    axis_dim = rot_dim // 2
    inv_freq = ROPE_THETA ** (-jnp.arange(0, axis_dim, 2, dtype=F32) / axis_dim)
    ang_row = inv_freq[:, None] * row[None, :]
    ang_col = inv_freq[:, None] * col[None, :]
    return jnp.stack([jnp.cos(ang_row), jnp.sin(ang_row), jnp.cos(ang_col), jnp.sin(ang_col)])


def _col(v):
    return v.astype(F32)[:, None]


def kernel(x, c, ctx, c_ctx, w_ada, b_ada, ln_g, ln_b, mlp_w1, mlp_w2, gqa_w_qkv, gqa_q_norm, gqa_k_norm, gqa_w_o, mla_w_in, mla_q_norm, mla_kv_norm, mla_w_uq, mla_w_ukv, mla_w_o):
    b, n_lat, d = x.shape
    n_ctx = ctx.shape[1]

    cv = jnp.concatenate([c, c_ctx[None, :], jnp.zeros((MOD_ROWS - b - 1, d), F32)], axis=0)
    mod = _modulation(cv, w_ada, b_ada)
    ln_cols = jnp.stack([ln_g[:, 0], ln_b[:, 0], ln_g[:, 1], ln_b[:, 1]], axis=1)
    ln_cols = jnp.broadcast_to(ln_cols[..., None], ln_cols.shape + (LANES,))

    lat_mod = None
    ctx_mod = b

    rope_gqa = _rope_tables_t(n_lat, GQA_HEAD_DIM)
    rope_mla = _rope_tables_t(n_lat, MLA_ROPE_DIM)

    w1_t, w2_t = _transpose_cast(mlp_w1), _transpose_cast(mlp_w2)
    gqa_w_t, gqa_wo_t = _transpose_cast(gqa_w_qkv), _transpose_cast(gqa_w_o)
    mla_wo_t = _transpose_cast(mla_w_o)
    n_b = mla_w_in.shape[0]
    mla_win_t = mla_w_in.swapaxes(1, 2).astype(BF16)
    mla_wuq_t = mla_w_uq.swapaxes(1, 2).astype(BF16)
    wukv = mla_w_ukv.reshape(n_b, MLA_KV_RANK, MLA_HEADS, MLA_NOPE_DIM + MLA_V_DIM)
    wukv = jnp.concatenate([wukv[..., :MLA_NOPE_DIM].reshape(n_b, MLA_KV_RANK, -1),
                            wukv[..., MLA_NOPE_DIM:].reshape(n_b, MLA_KV_RANK, -1)], axis=2)
    mla_wukv_t = wukv.swapaxes(1, 2).astype(BF16)

    x_s, xc_s = x, ctx.reshape(1, b * n_ctx, d)

    for i in range(DEPTH):
        need_ctx = i < DEPTH - 1
        first, last = i == 0, i == DEPTH - 1
        j = i // N_MIXERS
        if i % N_MIXERS == 0:
            wo_t = gqa_wo_t
            gq, gk = _col(gqa_q_norm[j]), _col(gqa_k_norm[j])
            q_l, k_l, v_l = _gqa_proj(x_s, first, mod, i, lat_mod, gqa_w_t, j, gq, gk, rope_gqa, LAT_TILE)
            q_c, k_c, v_c = _gqa_proj(xc_s, first, mod, i, ctx_mod, gqa_w_t, j, gq, gk, None, LAT_TILE)
            heads, group, dk_rows, dv = GQA_HEADS, GQA_HEADS // GQA_KV_HEADS, GQA_HEAD_DIM, GQA_HEAD_DIM
        else:
            wo_t = mla_wo_t
            gq, gkv = _col(mla_q_norm[j]), _col(mla_kv_norm[j])
            q_l, k_l, v_l = _mla_proj(x_s, mod, i, lat_mod, j, mla_win_t, gq, gkv, mla_wuq_t, mla_wukv_t,
                                      rope_mla, LAT_TILE)
            q_c, k_c, v_c = _mla_proj(xc_s, mod, i, ctx_mod, j, mla_win_t, gq, gkv, mla_wuq_t, mla_wukv_t,
                                      None, LAT_TILE)
            heads, group, dk_rows, dv = MLA_HEADS, 1, MLA_QK_PAD, MLA_V_DIM

        a_l = _attention(q_l, [(k_c, v_c, n_ctx), (k_l, v_l, n_lat)], b, heads, group, dk_rows, dv, LAT_TILE,
                         "attn_lat")
        x_s = _post(x_s, first, a_l, mod, ln_cols, i, lat_mod, wo_t, j, w1_t, w2_t, POST_TILE, last, "post_lat")
        if need_ctx:
            a_c = _ctx_attention(q_c, k_c, v_c, b, heads, group, dk_rows, dv)
            xc_s = _post(xc_s, first, a_c, mod, ln_cols, i, ctx_mod, wo_t, j, w1_t, w2_t, POST_TILE, False,
                         "post_ctx")

    return x_s
```

```python
import functools

import jax
import jax.numpy as jnp
from jax import lax
from jax.experimental import pallas as pl
from jax.experimental.pallas import tpu as pltpu

D_MODEL = 1024
DEPTH = 4
GRID_W = 64
N_MIXERS = 2
GQA_HEADS = 16
GQA_KV_HEADS = 4
GQA_HEAD_DIM = 64
MLA_HEADS = 16
MLA_Q_RANK = 384
MLA_KV_RANK = 256
MLA_NOPE_DIM = 64
MLA_ROPE_DIM = 32
MLA_V_DIM = 64
MLA_QK_DIM = MLA_NOPE_DIM + MLA_ROPE_DIM
MLA_QK_PAD = 128
FFN_HIDDEN = 4 * D_MODEL
ROPE_THETA = 10000.0
NORM_EPS = 1e-6
DEEPNORM_ALPHA = (2.0 * DEPTH) ** 0.25
LOG2E = 1.4426950408889634

HEAD_V_DIM = 64
ONES_ROWS = 16
V_ROWS = HEAD_V_DIM + ONES_ROWS

LAT_TILE = 512
POST_TILE = 1024
KV_CHUNK = 256
HEADS_PER_LOOP = 8
FFN_CHUNK = 1024
WEIGHT_TILE = 1024
MOD_ROWS = 16
PROJ_ROWS = 256
LANES = 128
VMEM_LIMIT_BYTES = 60 * 1024 * 1024

F32 = jnp.float32
BF16 = jnp.bfloat16


def _params(n_axes):
    return pltpu.CompilerParams(dimension_semantics=("arbitrary",) * n_axes,
                                vmem_limit_bytes=VMEM_LIMIT_BYTES)


def _transpose_cast_kernel(w_ref, o_ref):
    o_ref[0] = w_ref[0].T.astype(BF16)


def _transpose_cast(w):
    n_l, k, m = w.shape
    tk = WEIGHT_TILE if k % WEIGHT_TILE == 0 else WEIGHT_TILE // 2
    tm = WEIGHT_TILE if m % WEIGHT_TILE == 0 else WEIGHT_TILE // 2
    return pl.pallas_call(
        _transpose_cast_kernel,
        grid=(n_l, k // tk, m // tm),
        in_specs=[pl.BlockSpec((1, tk, tm), lambda l, i, j: (l, i, j))],
        out_specs=pl.BlockSpec((1, tm, tk), lambda l, i, j: (l, j, i)),
        out_shape=jax.ShapeDtypeStruct((n_l, m, k), BF16),
        compiler_params=_params(3),
        name="weight_transpose_cast",
    )(w)


def _layer_spec(w, idx, **kw):
    return pl.BlockSpec((1,) + w.shape[1:], lambda bi, ti: (idx, 0, 0), **kw)


def _mod_kernel(cv_ref, w_ref, b_ref, o_ref):
    c = cv_ref[...]
    s = c * jax.nn.sigmoid(c)
    w = w_ref[0]
    s_hi = s.astype(BF16)
    s_lo = (s - s_hi.astype(F32)).astype(BF16)
    w_hi = w.astype(BF16)
    w_lo = (w - w_hi.astype(F32)).astype(BF16)
    dot = functools.partial(jnp.dot, preferred_element_type=F32)
    o_ref[0] = dot(s_hi, w_hi) + (dot(s_lo, w_hi) + dot(s_hi, w_lo)) + b_ref[0]


def _modulation(cv, w_ada, b_ada):
    d = D_MODEL
    return pl.pallas_call(
        _mod_kernel,
        grid=(DEPTH, 6),
        in_specs=[pl.BlockSpec((MOD_ROWS, d), lambda i, j: (0, 0)),
                  pl.BlockSpec((1, d, d), lambda i, j: (i, 0, j)),
                  pl.BlockSpec((1, 1, d), lambda i, j: (i, 0, j))],
        out_specs=pl.BlockSpec((1, MOD_ROWS, d), lambda i, j: (i, 0, j)),
        out_shape=jax.ShapeDtypeStruct((DEPTH, MOD_ROWS, 6 * d), F32),
        compiler_params=_params(2),
        name="adaln_modulation",
    )(cv, w_ada, b_ada.reshape(DEPTH, 1, 6 * d))


def _columns(row):
    return jnp.broadcast_to(row, (LANES, row.shape[1])).T


def _cache_mod_columns(cols_ref, mod_refs, mod_row):
    d = D_MODEL

    @pl.when(pl.program_id(1) == 0)
    def _():
        r = pl.program_id(0) if mod_row is None else mod_row
        for i, mod_ref in enumerate(mod_refs):
            cols = _columns(mod_ref[0, pl.ds(r, 1), :])
            cols_ref[2 * i] = cols[0:d]
            cols_ref[2 * i + 1] = cols[d:2 * d]


def _wide(cols, t):
    return cols if t == LANES else jnp.concatenate([cols] * (t // LANES), axis=1)


def _modulate(x, cols_ref):
    t = x.shape[1]
    return x * (1.0 + _wide(cols_ref[1], t)) + _wide(cols_ref[0], t)


def _rms_rows(x, gain_col):
    ms = jnp.mean(x * x, axis=0, keepdims=True)
    return x * lax.rsqrt(ms + NORM_EPS) * gain_col


def _layer_norm_rows(z, g_col, b_col):
    mu = jnp.mean(z, axis=0, keepdims=True)
    zc = z - mu
    var = jnp.mean(zc * zc, axis=0, keepdims=True)
    return zc * lax.rsqrt(var + NORM_EPS) * g_col + b_col


def _rotate_rows(x, rope_ref):
    q = x.shape[0] // 4
    cr, sr, cc, sc = rope_ref[0], rope_ref[1], rope_ref[2], rope_ref[3]
    a1, a2, b1, b2 = x[0:q], x[q:2 * q], x[2 * q:3 * q], x[3 * q:4 * q]
    return jnp.concatenate([a1 * cr - a2 * sr, a2 * cr + a1 * sr,
                            b1 * cc - b2 * sc, b2 * cc + b1 * sc], axis=0)


def _store_values(v_ref, v, n_heads):
    ones = jnp.ones((ONES_ROWS, v.shape[1]), BF16)
    for j in range(n_heads):
        v_ref[0, j * V_ROWS:j * V_ROWS + HEAD_V_DIM, :] = v[j * HEAD_V_DIM:(j + 1) * HEAD_V_DIM].astype(BF16)
        v_ref[0, j * V_ROWS + HEAD_V_DIM:(j + 1) * V_ROWS, :] = ones


def _gqa_proj_kernel(x_ref, mod_ref, w_ref, gq_ref, gk_ref, *rest, rotate, natural_in, mod_row):
    rest = list(rest)
    rope_ref = rest.pop(0) if rotate else None
    cols_ref = rest.pop()
    q_ref, k_ref, v_ref = rest[:3]
    d = D_MODEL
    hd = GQA_HEAD_DIM
    nq = GQA_HEADS * hd
    nk = GQA_KV_HEADS * hd
    if natural_in:
        x = x_ref[0]
        r = pl.program_id(0) if mod_row is None else mod_row
        row = mod_ref[0, pl.ds(r, 1), :]
        h = (x * (1.0 + row[:, d:2 * d]) + row[:, 0:d]).astype(BF16)
        rest[3][0] = x.T
        dims = (((1,), (1,)), ((), ()))
    else:
        _cache_mod_columns(cols_ref, [mod_ref], mod_row)
        h = _modulate(x_ref[0], cols_ref).astype(BF16)
        dims = (((1,), (0,)), ((), ()))
    gq = gq_ref[...]
    gk = gk_ref[...]
    q_scale = hd ** -0.5 * LOG2E
    heads_per_step = PROJ_ROWS // hd

    def project(row0, rows):
        return lax.dot_general(w_ref[0, row0:row0 + rows, :], h, dims, preferred_element_type=F32)

    for g in range(nq // PROJ_ROWS):
        p = project(g * PROJ_ROWS, PROJ_ROWS)
        for i in range(heads_per_step):
            qh = _rms_rows(p[i * hd:(i + 1) * hd], gq)
            if rotate:
                qh = _rotate_rows(qh, rope_ref)
            row = g * PROJ_ROWS + i * hd
            q_ref[0, row:row + hd, :] = (qh * q_scale).astype(BF16)
    p = project(nq, nk)
    ks = []
    for j in range(GQA_KV_HEADS):
        kh = _rms_rows(p[j * hd:(j + 1) * hd], gk)
        if rotate:
            kh = _rotate_rows(kh, rope_ref)
        ks.append(kh)
    kt = jnp.concatenate(ks, axis=0).T
    for j in range(GQA_KV_HEADS):
        k_ref[0, j] = kt[:, j * hd:(j + 1) * hd].astype(BF16)
    _store_values(v_ref, project(nq + nk, nk), GQA_KV_HEADS)


def _gqa_proj(x, natural_in, mod, layer, mod_row, w_t, w_idx, gq_col, gk_col, rope, tile):
    if natural_in:
        b, n, d = x.shape
        x_spec = pl.BlockSpec((1, tile, d), lambda bi, ti: (bi, ti, 0))
    else:
        b, d, n = x.shape
        x_spec = pl.BlockSpec((1, d, tile), lambda bi, ti: (bi, 0, ti))
    hd = GQA_HEAD_DIM
    nq = GQA_HEADS * hd
    rotate = rope is not None
    in_specs = [x_spec,
                pl.BlockSpec((1, MOD_ROWS, 2 * d), lambda bi, ti: (layer, 0, 0)),
                _layer_spec(w_t, w_idx),
                pl.BlockSpec(gq_col.shape, lambda bi, ti: (0, 0)),
                pl.BlockSpec(gk_col.shape, lambda bi, ti: (0, 0))]
    args = [x, mod, w_t, gq_col, gk_col]
    if rotate:
        in_specs.append(pl.BlockSpec((4, hd // 4, tile), lambda bi, ti: (0, 0, ti)))
        args.append(rope)
    out_specs = [pl.BlockSpec((1, nq, tile), lambda bi, ti: (bi, 0, ti)),
                 pl.BlockSpec((1, GQA_KV_HEADS, tile, hd), lambda bi, ti: (bi, 0, ti, 0)),
                 pl.BlockSpec((1, GQA_KV_HEADS * V_ROWS, tile), lambda bi, ti: (bi, 0, ti))]
    out_shape = [jax.ShapeDtypeStruct((b, nq, n), BF16),
                 jax.ShapeDtypeStruct((b, GQA_KV_HEADS, n, hd), BF16),
                 jax.ShapeDtypeStruct((b, GQA_KV_HEADS * V_ROWS, n), BF16)]
    if natural_in:
        out_specs.append(pl.BlockSpec((1, d, tile), lambda bi, ti: (bi, 0, ti)))
        out_shape.append(jax.ShapeDtypeStruct((b, d, n), F32))
    return pl.pallas_call(
        functools.partial(_gqa_proj_kernel, rotate=rotate, natural_in=natural_in, mod_row=mod_row),
        grid=(b, n // tile),
        in_specs=in_specs,
        out_specs=out_specs,
        out_shape=out_shape,
        scratch_shapes=[pltpu.VMEM((2, d, LANES), F32)],
        compiler_params=_params(2),
        name="gqa_proj_lat" if rotate else "gqa_proj_ctx",
    )(*args)


def _mla_proj_kernel(x_ref, mod_ref, win_ref, gq_ref, gkv_ref, wuq_ref, wukv_ref, *rest, rotate, mod_row):
    if rotate:
        rope_ref, q_ref, k_ref, v_ref, cols_ref = rest
    else:
        q_ref, k_ref, v_ref, cols_ref = rest
    _cache_mod_columns(cols_ref, [mod_ref], mod_row)
    nh, nope, rd, pad = MLA_HEADS, MLA_NOPE_DIM, MLA_ROPE_DIM, MLA_QK_PAD
    t = x_ref.shape[2]
    h = _modulate(x_ref[0], cols_ref).astype(BF16)
    p = jnp.dot(win_ref[0], h, preferred_element_type=F32)
    cq = _rms_rows(p[0:MLA_Q_RANK], gq_ref[...]).astype(BF16)
    ckv = _rms_rows(p[MLA_Q_RANK:MLA_Q_RANK + MLA_KV_RANK], gkv_ref[...]).astype(BF16)
    k_pe = p[MLA_Q_RANK + MLA_KV_RANK:MLA_Q_RANK + MLA_KV_RANK + rd]
    if rotate:
        k_pe = _rotate_rows(k_pe, rope_ref)
    q = jnp.dot(wuq_ref[0], cq, preferred_element_type=F32)
    kv = jnp.dot(wukv_ref[0], ckv, preferred_element_type=F32)
    q_scale = MLA_QK_DIM ** -0.5 * LOG2E
    zeros = jnp.zeros((pad - nope - rd, t), F32)
    qk = MLA_QK_DIM
    for i in range(nh):
        q_pe = q[i * qk + nope:(i + 1) * qk]
        if rotate:
            q_pe = _rotate_rows(q_pe, rope_ref)
        qh = jnp.concatenate([q[i * qk:i * qk + nope], q_pe], axis=0) * q_scale
        q_ref[0, i * pad:(i + 1) * pad, :] = jnp.concatenate([qh, zeros], axis=0).astype(BF16)
        kh = jnp.concatenate([kv[i * nope:(i + 1) * nope], k_pe, zeros], axis=0)
        k_ref[0, i] = kh.T.astype(BF16)
    _store_values(v_ref, kv[nh * nope:], nh)


def _mla_proj(x_t, mod, layer, mod_row, w_idx, win_t, gq_col, gkv_col, wuq_t, wukv_t, rope, tile):
    b, d, n = x_t.shape
    nh, pad = MLA_HEADS, MLA_QK_PAD
    rotate = rope is not None
    in_specs = [pl.BlockSpec((1, d, tile), lambda bi, ti: (bi, 0, ti)),
                pl.BlockSpec((1, MOD_ROWS, 2 * d), lambda bi, ti: (layer, 0, 0)),
                _layer_spec(win_t, w_idx),
                pl.BlockSpec(gq_col.shape, lambda bi, ti: (0, 0)),
                pl.BlockSpec(gkv_col.shape, lambda bi, ti: (0, 0)),
                _layer_spec(wuq_t, w_idx),
                _layer_spec(wukv_t, w_idx)]
    args = [x_t, mod, win_t, gq_col, gkv_col, wuq_t, wukv_t]
    if rotate:
        in_specs.append(pl.BlockSpec((4, MLA_ROPE_DIM // 4, tile), lambda bi, ti: (0, 0, ti)))
        args.append(rope)
    return pl.pallas_call(
        functools.partial(_mla_proj_kernel, rotate=rotate, mod_row=mod_row),
        grid=(b, n // tile),
        in_specs=in_specs,
        scratch_shapes=[pltpu.VMEM((2, d, LANES), F32)],
        out_specs=[pl.BlockSpec((1, nh * pad, tile), lambda bi, ti: (bi, 0, ti)),
                   pl.BlockSpec((1, nh, tile, pad), lambda bi, ti: (bi, 0, ti, 0)),
                   pl.BlockSpec((1, nh * V_ROWS, tile), lambda bi, ti: (bi, 0, ti))],
        out_shape=[jax.ShapeDtypeStruct((b, nh * pad, n), BF16),
                   jax.ShapeDtypeStruct((b, nh, n, pad), BF16),
                   jax.ShapeDtypeStruct((b, nh * V_ROWS, n), BF16)],
        compiler_params=_params(2),
        name="mla_proj_lat" if rotate else "mla_proj_ctx",
    )(*args)


def _attn_kernel(q_ref, *rest, n_heads, group, dk_rows, dv):
    n_src = (len(rest) - 5) // 3
    k_refs = rest[0:2 * n_src:2]
    v_refs = rest[1:2 * n_src:2]
    qn_ref = rest[2 * n_src]
    kn_refs = rest[2 * n_src + 1:3 * n_src + 1]
    o_ref = rest[3 * n_src + 1]
    s_refs = rest[3 * n_src + 2:3 * n_src + 4]
    m_ref = rest[3 * n_src + 4]
    chunks = []
    base = 0
    for src, k_ref in enumerate(k_refs):
        for lo in range(0, k_ref.shape[2], KV_CHUNK):
            chunks.append((src, lo, base + lo))
        base += k_ref.shape[2]

    def head_of_this_step(h, slot):
        kvh = h // group
        q = q_ref[0, pl.ds(pl.multiple_of(h * dk_rows, dk_rows), dk_rows), :]
        return q, (lambda src, lo: k_refs[src][0, kvh, lo:lo + KV_CHUNK, :]), slot

    def head_0_of_next_step():
        return qn_ref[0], (lambda src, lo: kn_refs[src][0, 0, lo:lo + KV_CHUNK, :]), 0

    def step(cur, nxt):
        q, load_keys, slot_n = nxt
        if cur is not None:
            h_c, slot_c, m_c = cur
            v_row = pl.multiple_of((h_c // group) * V_ROWS, ONES_ROWS)
        m_n = None
        acc = None
        for src, lo, glo in chunks:
            s = jnp.dot(load_keys(src, lo), q, preferred_element_type=F32)
            s_refs[slot_n][glo:glo + KV_CHUNK, :] = s
            ms = jnp.max(s, axis=0, keepdims=True)
            m_n = ms if m_n is None else jnp.maximum(m_n, ms)
            if cur is not None:
                p = jnp.exp2(s_refs[slot_c][glo:glo + KV_CHUNK, :] - m_c).astype(BF16)
                part = jnp.dot(v_refs[src][0, pl.ds(v_row, V_ROWS), lo:lo + KV_CHUNK], p,
                               preferred_element_type=F32)
                acc = part if acc is None else acc + part
        if cur is not None:
            out = acc[0:dv] / acc[dv:dv + 1]
            o_ref[0, pl.ds(pl.multiple_of(h_c * dv, dv), dv), :] = out.astype(BF16)
        return m_n

    @pl.when(jnp.logical_and(pl.program_id(0) == 0, pl.program_id(1) == 0))
    def _():
        m_ref[...] = step(None, head_of_this_step(0, 0))

    def run(i, m):
        h0 = i * HEADS_PER_LOOP
        for t in range(HEADS_PER_LOOP):
            m = step((h0 + t, t % 2, m), head_of_this_step(h0 + t + 1, (t + 1) % 2))
        return m

    n_loop = n_heads // HEADS_PER_LOOP - 1
    m = lax.fori_loop(0, jnp.minimum(pl.program_id(0) + n_loop, n_loop), run, m_ref[...])
    h0 = n_loop * HEADS_PER_LOOP
    for t in range(HEADS_PER_LOOP - 1):
        m = step((h0 + t, t % 2, m), head_of_this_step(h0 + t + 1, (t + 1) % 2))
    m_ref[...] = step((n_heads - 1, 1, m), head_0_of_next_step())


def _attention(q_t, kv_sources, batch, n_heads, group, dk_rows, dv, tq, name):
    qb, qrows, nq = q_t.shape
    tiles = nq // tq
    grid = (batch, tiles)

    def next_step(bi, ti):
        flat = jnp.minimum(bi * tiles + ti + 1, batch * tiles - 1)
        return flat // tiles, flat % tiles

    q_index = lambda bi, ti: (bi, 0, ti)
    qn_index = lambda bi, ti: (next_step(bi, ti)[0], 0, next_step(bi, ti)[1])
    in_specs = [pl.BlockSpec((1, qrows, tq), q_index)]
    args = [q_t]
    next_specs = [pl.BlockSpec((1, dk_rows, tq), qn_index)]
    next_args = [q_t]
    n_keys = 0
    for k, v, keys in kv_sources:
        if k.shape[0] == 1:
            k_index, v_index = (lambda bi, ti: (0, 0, bi, 0)), (lambda bi, ti: (0, 0, bi))
            kn_index = lambda bi, ti: (0, 0, next_step(bi, ti)[0], 0)
        else:
            k_index, v_index = (lambda bi, ti: (bi, 0, 0, 0)), (lambda bi, ti: (bi, 0, 0))
            kn_index = lambda bi, ti: (next_step(bi, ti)[0], 0, 0, 0)
        in_specs.append(pl.BlockSpec((1, k.shape[1], keys, k.shape[3]), k_index))
        in_specs.append(pl.BlockSpec((1, v.shape[1], keys), v_index))
        args += [k, v]
        next_specs.append(pl.BlockSpec((1, 1, keys, k.shape[3]), kn_index))
        next_args.append(k)
        n_keys += keys
    return pl.pallas_call(
        functools.partial(_attn_kernel, n_heads=n_heads, group=group, dk_rows=dk_rows, dv=dv),
        grid=grid,
        in_specs=in_specs + next_specs,
        out_specs=pl.BlockSpec((1, n_heads * dv, tq), q_index),
        out_shape=jax.ShapeDtypeStruct((qb, n_heads * dv, nq), BF16),
        scratch_shapes=[pltpu.VMEM((n_keys, tq), F32), pltpu.VMEM((n_keys, tq), F32),
                        pltpu.VMEM((1, tq), F32)],
        compiler_params=_params(2),
        name=name,
    )(*args, *next_args)


def _ctx_attn_kernel(q_ref, k_ref, v_ref, o_ref, s_ref, *, n_heads, group, dk_rows, dv):
    maxes = []
    for h in range(n_heads):
        q = q_ref[0, h * dk_rows:(h + 1) * dk_rows, :]
        s = jnp.dot(k_ref[0, h // group], q, preferred_element_type=F32)
        s_ref[h] = s
        maxes.append(jnp.max(s, axis=0, keepdims=True))
    for h in range(n_heads):
        p = jnp.exp2(s_ref[h] - maxes[h]).astype(BF16)
        kvh = h // group
        acc = jnp.dot(v_ref[0, kvh * V_ROWS:(kvh + 1) * V_ROWS, :], p, preferred_element_type=F32)
        o_ref[0, h * dv:(h + 1) * dv, :] = (acc[0:dv] / acc[dv:dv + 1]).astype(BF16)


def _ctx_attention(q_t, k, v, batch, n_heads, group, dk_rows, dv):
    _, qrows, total = q_t.shape
    n = total // batch
    return pl.pallas_call(
        functools.partial(_ctx_attn_kernel, n_heads=n_heads, group=group, dk_rows=dk_rows, dv=dv),
        grid=(batch,),
        in_specs=[pl.BlockSpec((1, qrows, n), lambda bi: (0, 0, bi)),
                  pl.BlockSpec((1, k.shape[1], n, k.shape[3]), lambda bi: (0, 0, bi, 0)),
                  pl.BlockSpec((1, v.shape[1], n), lambda bi: (0, 0, bi))],
        out_specs=pl.BlockSpec((1, n_heads * dv, n), lambda bi: (0, 0, bi)),
        out_shape=jax.ShapeDtypeStruct((1, n_heads * dv, total), BF16),
        scratch_shapes=[pltpu.VMEM((n_heads, n, n), F32)],
        compiler_params=_params(1),
        name="attn_ctx",
    )(q_t, k, v)


def _post_kernel(x_ref, a_ref, moda_ref, modm_ref, ln_ref, wo_ref, w1_ref, w2_ref, o_ref, cols_ref, *,
                 natural_out, mod_row):
    d = D_MODEL
    t = x_ref.shape[2]
    half = t // 2
    pieces = half // LANES
    n_chunks = FFN_HIDDEN // FFN_CHUNK
    _cache_mod_columns(cols_ref, [moda_ref, modm_ref], mod_row)

    def col(i, width):
        return _wide(cols_ref[i] if i < 4 else ln_ref[0, i - 4], width)

    def merge_attn(lo, width, y):
        x1 = _layer_norm_rows(DEEPNORM_ALPHA * x_ref[0, :, lo:lo + width] + col(0, width) * y,
                              col(4, width), col(5, width))
        return x1, (x1 * (1.0 + col(2, width)) + col(1, width)).astype(BF16)

    def mlp_chunk(c, h):
        u = jnp.dot(w1_ref[0, c * FFN_CHUNK:(c + 1) * FFN_CHUNK, :], h, preferred_element_type=F32)
        u = jnp.maximum(u, 0.0)
        u = (u * u).astype(BF16)
        return jnp.dot(w2_ref[0, :, c * FFN_CHUNK:(c + 1) * FFN_CHUNK], u, preferred_element_type=F32)

    def merge_mlp(lo, x1, acc):
        x2 = _layer_norm_rows(DEEPNORM_ALPHA * x1 + col(3, LANES) * acc, col(6, LANES), col(7, LANES))
        if natural_out:
            o_ref[0, lo:lo + LANES, :] = x2.T
        else:
            o_ref[0, :, lo:lo + LANES] = x2

    y_a = jnp.dot(wo_ref[0], a_ref[0, :, 0:half], preferred_element_type=F32)
    y_b = jnp.dot(wo_ref[0], a_ref[0, :, half:t], preferred_element_type=F32)
    x1_a, h_a = merge_attn(0, half, y_a)
    acc_a = None
    x1_b, h_b = [], []
    for c in range(n_chunks):
        part = mlp_chunk(c, h_a)
        acc_a = part if acc_a is None else acc_a + part
        if c < pieces:
            x1, h = merge_attn(half + c * LANES, LANES, y_b[:, c * LANES:(c + 1) * LANES])
            x1_b.append(x1)
            h_b.append(h)
    h_b = jnp.concatenate(h_b, axis=1)
    acc_b = None
    for c in range(n_chunks):
        part = mlp_chunk(c, h_b)
        acc_b = part if acc_b is None else acc_b + part
        if c < pieces:
            merge_mlp(c * LANES, x1_a[:, c * LANES:(c + 1) * LANES], acc_a[:, c * LANES:(c + 1) * LANES])
    for j in range(pieces):
        merge_mlp(half + j * LANES, x1_b[j], acc_b[:, j * LANES:(j + 1) * LANES])


def _post(x_t, attn_t, mod, ln_cols, layer, mod_row, wo_t, wo_idx, w1_t, w2_t, tile, natural_out, name):
    b, d, n = x_t.shape
    single = dict(pipeline_mode=pl.Buffered(1))
    if natural_out:
        out_spec = pl.BlockSpec((1, tile, d), lambda bi, ti: (bi, ti, 0))
        out_shape = jax.ShapeDtypeStruct((b, n, d), F32)
    else:
        out_spec = pl.BlockSpec((1, d, tile), lambda bi, ti: (bi, 0, ti))
        out_shape = jax.ShapeDtypeStruct((b, d, n), F32)
    return pl.pallas_call(
        functools.partial(_post_kernel, natural_out=natural_out, mod_row=mod_row),
        grid=(b, n // tile),
        in_specs=[pl.BlockSpec((1, d, tile), lambda bi, ti: (bi, 0, ti)),
                  pl.BlockSpec((1, d, tile), lambda bi, ti: (bi, 0, ti)),
                  pl.BlockSpec((1, MOD_ROWS, 2 * d), lambda bi, ti: (layer, 0, 1)),
                  pl.BlockSpec((1, MOD_ROWS, 2 * d), lambda bi, ti: (layer, 0, 2)),
                  pl.BlockSpec((1, 4, d, LANES), lambda bi, ti: (layer, 0, 0, 0), **single),
                  _layer_spec(wo_t, wo_idx, **single),
                  _layer_spec(w1_t, layer, **single),
                  _layer_spec(w2_t, layer, **single)],
        out_specs=out_spec,
        out_shape=out_shape,
        scratch_shapes=[pltpu.VMEM((4, d, LANES), F32)],
        compiler_params=_params(2),
        name=name,
    )(x_t, attn_t, mod, mod, ln_cols, wo_t, w1_t, w2_t)


def _rope_tables_t(n_tok, rot_dim):
    rows = n_tok // GRID_W
    row = jnp.broadcast_to(jnp.arange(rows, dtype=F32)[:, None], (rows, GRID_W)).reshape(-1)
    col = jnp.broadcast_to(jnp.arange(GRID_W, dtype=F32)[None, :], (rows, GRID_W)).reshape(-1)
    axis_dim = rot_dim // 2
    inv_freq = ROPE_THETA ** (-jnp.arange(0, axis_dim, 2, dtype=F32) / axis_dim)
    ang_row = inv_freq[:, None] * row[None, :]
    ang_col = inv_freq[:, None] * col[None, :]
    return jnp.stack([jnp.cos(ang_row), jnp.sin(ang_row), jnp.cos(ang_col), jnp.sin(ang_col)])


def _col(v):
    return v.astype(F32)[:, None]


def kernel(x, c, ctx, c_ctx, w_ada, b_ada, ln_g, ln_b, mlp_w1, mlp_w2, gqa_w_qkv, gqa_q_norm, gqa_k_norm, gqa_w_o, mla_w_in, mla_q_norm, mla_kv_norm, mla_w_uq, mla_w_ukv, mla_w_o):
    b, n_lat, d = x.shape
    n_ctx = ctx.shape[1]

    cv = jnp.concatenate([c, c_ctx[None, :], jnp.zeros((MOD_ROWS - b - 1, d), F32)], axis=0)
    mod = _modulation(cv, w_ada, b_ada)
    ln_cols = jnp.stack([ln_g[:, 0], ln_b[:, 0], ln_g[:, 1], ln_b[:, 1]], axis=1)
    ln_cols = jnp.broadcast_to(ln_cols[..., None], ln_cols.shape + (LANES,))

    lat_mod = None
    ctx_mod = b

    rope_gqa = _rope_tables_t(n_lat, GQA_HEAD_DIM)
    rope_mla = _rope_tables_t(n_lat, MLA_ROPE_DIM)

    w1_t, w2_t = _transpose_cast(mlp_w1), _transpose_cast(mlp_w2)
    gqa_w_t, gqa_wo_t = _transpose_cast(gqa_w_qkv), _transpose_cast(gqa_w_o)
    mla_wo_t = _transpose_cast(mla_w_o)
    n_b = mla_w_in.shape[0]
    mla_win_t = mla_w_in.swapaxes(1, 2).astype(BF16)
    mla_wuq_t = mla_w_uq.swapaxes(1, 2).astype(BF16)
    wukv = mla_w_ukv.reshape(n_b, MLA_KV_RANK, MLA_HEADS, MLA_NOPE_DIM + MLA_V_DIM)
    wukv = jnp.concatenate([wukv[..., :MLA_NOPE_DIM].reshape(n_b, MLA_KV_RANK, -1),
                            wukv[..., MLA_NOPE_DIM:].reshape(n_b, MLA_KV_RANK, -1)], axis=2)
    mla_wukv_t = wukv.swapaxes(1, 2).astype(BF16)

    xc_nat = ctx.reshape(1, b * n_ctx, d)
    x_t = xc_t = None

    for i in range(DEPTH):
        need_ctx = i < DEPTH - 1
        last = i == DEPTH - 1
        j = i // N_MIXERS
        if i % N_MIXERS == 0:
            wo_t = gqa_wo_t
            gq, gk = _col(gqa_q_norm[j]), _col(gqa_k_norm[j])
            if i == 0:
                q_l, k_l, v_l, x_t = _gqa_proj(x, True, mod, i, lat_mod, gqa_w_t, j, gq, gk, rope_gqa, LAT_TILE)
                q_c, k_c, v_c, xc_t = _gqa_proj(xc_nat, True, mod, i, ctx_mod, gqa_w_t, j, gq, gk, None, LAT_TILE)
            else:
                q_l, k_l, v_l = _gqa_proj(x_t, False, mod, i, lat_mod, gqa_w_t, j, gq, gk, rope_gqa, LAT_TILE)
                q_c, k_c, v_c = _gqa_proj(xc_t, False, mod, i, ctx_mod, gqa_w_t, j, gq, gk, None, LAT_TILE)
            heads, group, dk_rows, dv = GQA_HEADS, GQA_HEADS // GQA_KV_HEADS, GQA_HEAD_DIM, GQA_HEAD_DIM
        else:
            wo_t = mla_wo_t
            gq, gkv = _col(mla_q_norm[j]), _col(mla_kv_norm[j])
            q_l, k_l, v_l = _mla_proj(x_t, mod, i, lat_mod, j, mla_win_t, gq, gkv, mla_wuq_t, mla_wukv_t,
                                      rope_mla, LAT_TILE)
            q_c, k_c, v_c = _mla_proj(xc_t, mod, i, ctx_mod, j, mla_win_t, gq, gkv, mla_wuq_t, mla_wukv_t,
                                      None, LAT_TILE)
            heads, group, dk_rows, dv = MLA_HEADS, 1, MLA_QK_PAD, MLA_V_DIM

        a_l = _attention(q_l, [(k_c, v_c, n_ctx), (k_l, v_l, n_lat)], b, heads, group, dk_rows, dv, LAT_TILE,
                         "attn_lat")
        x_t = _post(x_t, a_l, mod, ln_cols, i, lat_mod, wo_t, j, w1_t, w2_t, POST_TILE, last, "post_lat")
        if need_ctx:
            a_c = _ctx_attention(q_c, k_c, v_c, b, heads, group, dk_rows, dv)
            xc_t = _post(xc_t, a_c, mod, ln_cols, i, ctx_mod, wo_t, j, w1_t, w2_t, POST_TILE, False, "post_ctx")

    return x_t
```

```python
import functools

import jax
import jax.numpy as jnp
from jax import lax
from jax.experimental import pallas as pl
from jax.experimental.pallas import tpu as pltpu

D_MODEL = 1024
DEPTH = 4
GRID_W = 64
N_MIXERS = 2
GQA_HEADS = 16
GQA_KV_HEADS = 4
GQA_HEAD_DIM = 64
MLA_HEADS = 16
MLA_Q_RANK = 384
MLA_KV_RANK = 256
MLA_NOPE_DIM = 64
MLA_ROPE_DIM = 32
MLA_V_DIM = 64
MLA_QK_DIM = MLA_NOPE_DIM + MLA_ROPE_DIM
MLA_QK_PAD = 128
FFN_HIDDEN = 4 * D_MODEL
ROPE_THETA = 10000.0
NORM_EPS = 1e-6
DEEPNORM_ALPHA = (2.0 * DEPTH) ** 0.25
LOG2E = 1.4426950408889634

HEAD_V_DIM = 64
ONES_ROWS = 16
V_ROWS = HEAD_V_DIM + ONES_ROWS

LAT_TILE = 512
POST_TILE = 1024
KV_CHUNK = 256
HEADS_PER_LOOP = 8
Q_TILES_PER_STEP = 2
FFN_CHUNK = 1024
WEIGHT_TILE = 1024
MOD_ROWS = 16
PROJ_ROWS = 256
LANES = 128
VMEM_LIMIT_BYTES = 60 * 1024 * 1024

F32 = jnp.float32
BF16 = jnp.bfloat16


def _params(n_axes):
    return pltpu.CompilerParams(dimension_semantics=("arbitrary",) * n_axes,
                                vmem_limit_bytes=VMEM_LIMIT_BYTES)


def _transpose_cast_kernel(w_ref, o_ref):
    o_ref[0] = w_ref[0].T.astype(BF16)


def _transpose_cast(w):
    n_l, k, m = w.shape
    tk = WEIGHT_TILE if k % WEIGHT_TILE == 0 else WEIGHT_TILE // 2
    tm = WEIGHT_TILE if m % WEIGHT_TILE == 0 else WEIGHT_TILE // 2
    return pl.pallas_call(
        _transpose_cast_kernel,
        grid=(n_l, k // tk, m // tm),
        in_specs=[pl.BlockSpec((1, tk, tm), lambda l, i, j: (l, i, j))],
        out_specs=pl.BlockSpec((1, tm, tk), lambda l, i, j: (l, j, i)),
        out_shape=jax.ShapeDtypeStruct((n_l, m, k), BF16),
        compiler_params=_params(3),
        name="weight_transpose_cast",
    )(w)


def _layer_spec(w, idx, **kw):
    return pl.BlockSpec((1,) + w.shape[1:], lambda bi, ti: (idx, 0, 0), **kw)


def _mod_kernel(cv_ref, w_ref, b_ref, o_ref):
    c = cv_ref[...]
    s = c * jax.nn.sigmoid(c)
    w = w_ref[0]
    s_hi = s.astype(BF16)
    s_lo = (s - s_hi.astype(F32)).astype(BF16)
    w_hi = w.astype(BF16)
    w_lo = (w - w_hi.astype(F32)).astype(BF16)
    dot = functools.partial(jnp.dot, preferred_element_type=F32)
    o_ref[0] = dot(s_hi, w_hi) + (dot(s_lo, w_hi) + dot(s_hi, w_lo)) + b_ref[0]


def _modulation(cv, w_ada, b_ada):
    d = D_MODEL
    return pl.pallas_call(
        _mod_kernel,
        grid=(DEPTH, 6),
        in_specs=[pl.BlockSpec((MOD_ROWS, d), lambda i, j: (0, 0)),
                  pl.BlockSpec((1, d, d), lambda i, j: (i, 0, j)),
                  pl.BlockSpec((1, 1, d), lambda i, j: (i, 0, j))],
        out_specs=pl.BlockSpec((1, MOD_ROWS, d), lambda i, j: (i, 0, j)),
        out_shape=jax.ShapeDtypeStruct((DEPTH, MOD_ROWS, 6 * d), F32),
        compiler_params=_params(2),
        name="adaln_modulation",
    )(cv, w_ada, b_ada.reshape(DEPTH, 1, 6 * d))


def _columns(row):
    return jnp.broadcast_to(row, (LANES, row.shape[1])).T


def _cache_mod_columns(cols_ref, mod_refs, mod_row):
    d = D_MODEL

    @pl.when(pl.program_id(1) == 0)
    def _():
        r = pl.program_id(0) if mod_row is None else mod_row
        for i, mod_ref in enumerate(mod_refs):
            cols = _columns(mod_ref[0, pl.ds(r, 1), :])
            cols_ref[2 * i] = cols[0:d]
            cols_ref[2 * i + 1] = cols[d:2 * d]


def _wide(cols, t):
    return cols if t == LANES else jnp.concatenate([cols] * (t // LANES), axis=1)


def _modulate(x, cols_ref):
    t = x.shape[1]
    return x * (1.0 + _wide(cols_ref[1], t)) + _wide(cols_ref[0], t)


def _rms_rows(x, gain_col):
    ms = jnp.mean(x * x, axis=0, keepdims=True)
    return x * lax.rsqrt(ms + NORM_EPS) * gain_col


def _layer_norm_rows(z, g_col, b_col):
    mu = jnp.mean(z, axis=0, keepdims=True)
    zc = z - mu
    var = jnp.mean(zc * zc, axis=0, keepdims=True)
    return zc * lax.rsqrt(var + NORM_EPS) * g_col + b_col


def _rotate_rows(x, rope_ref):
    q = x.shape[0] // 4
    cr, sr, cc, sc = rope_ref[0], rope_ref[1], rope_ref[2], rope_ref[3]
    a1, a2, b1, b2 = x[0:q], x[q:2 * q], x[2 * q:3 * q], x[3 * q:4 * q]
    return jnp.concatenate([a1 * cr - a2 * sr, a2 * cr + a1 * sr,
                            b1 * cc - b2 * sc, b2 * cc + b1 * sc], axis=0)


def _store_values(v_ref, v, n_heads):
    ones = jnp.ones((ONES_ROWS, v.shape[1]), BF16)
    for j in range(n_heads):
        v_ref[0, j * V_ROWS:j * V_ROWS + HEAD_V_DIM, :] = v[j * HEAD_V_DIM:(j + 1) * HEAD_V_DIM].astype(BF16)
        v_ref[0, j * V_ROWS + HEAD_V_DIM:(j + 1) * V_ROWS, :] = ones


def _gqa_proj_kernel(x_ref, mod_ref, w_ref, gq_ref, gk_ref, *rest, rotate, natural_in, mod_row):
    rest = list(rest)
    rope_ref = rest.pop(0) if rotate else None
    cols_ref = rest.pop()
    q_ref, k_ref, v_ref = rest[:3]
    d = D_MODEL
    hd = GQA_HEAD_DIM
    nq = GQA_HEADS * hd
    nk = GQA_KV_HEADS * hd
    if natural_in:
        x = x_ref[0]
        r = pl.program_id(0) if mod_row is None else mod_row
        row = mod_ref[0, pl.ds(r, 1), :]
        h = (x * (1.0 + row[:, d:2 * d]) + row[:, 0:d]).astype(BF16)
        rest[3][0] = x.T
        dims = (((1,), (1,)), ((), ()))
    else:
        _cache_mod_columns(cols_ref, [mod_ref], mod_row)
        h = _modulate(x_ref[0], cols_ref).astype(BF16)
        dims = (((1,), (0,)), ((), ()))
    gq = gq_ref[...]
    gk = gk_ref[...]
    q_scale = hd ** -0.5 * LOG2E
    heads_per_step = PROJ_ROWS // hd

    def project(row0, rows):
        return lax.dot_general(w_ref[0, row0:row0 + rows, :], h, dims, preferred_element_type=F32)

    for g in range(nq // PROJ_ROWS):
        p = project(g * PROJ_ROWS, PROJ_ROWS)
        for i in range(heads_per_step):
            qh = _rms_rows(p[i * hd:(i + 1) * hd], gq)
            if rotate:
                qh = _rotate_rows(qh, rope_ref)
            row = g * PROJ_ROWS + i * hd
            q_ref[0, row:row + hd, :] = (qh * q_scale).astype(BF16)
    p = project(nq, nk)
    ks = []
    for j in range(GQA_KV_HEADS):
        kh = _rms_rows(p[j * hd:(j + 1) * hd], gk)
        if rotate:
            kh = _rotate_rows(kh, rope_ref)
        ks.append(kh)
    kt = jnp.concatenate(ks, axis=0).T
    for j in range(GQA_KV_HEADS):
        k_ref[0, j] = kt[:, j * hd:(j + 1) * hd].astype(BF16)
    _store_values(v_ref, project(nq + nk, nk), GQA_KV_HEADS)


def _gqa_proj(x, natural_in, mod, layer, mod_row, w_t, w_idx, gq_col, gk_col, rope, tile):
    if natural_in:
        b, n, d = x.shape
        x_spec = pl.BlockSpec((1, tile, d), lambda bi, ti: (bi, ti, 0))
    else:
        b, d, n = x.shape
        x_spec = pl.BlockSpec((1, d, tile), lambda bi, ti: (bi, 0, ti))
    hd = GQA_HEAD_DIM
    nq = GQA_HEADS * hd
    rotate = rope is not None
    in_specs = [x_spec,
                pl.BlockSpec((1, MOD_ROWS, 2 * d), lambda bi, ti: (layer, 0, 0)),
                _layer_spec(w_t, w_idx),
                pl.BlockSpec(gq_col.shape, lambda bi, ti: (0, 0)),
                pl.BlockSpec(gk_col.shape, lambda bi, ti: (0, 0))]
    args = [x, mod, w_t, gq_col, gk_col]
    if rotate:
        in_specs.append(pl.BlockSpec((4, hd // 4, tile), lambda bi, ti: (0, 0, ti)))
        args.append(rope)
    out_specs = [pl.BlockSpec((1, nq, tile), lambda bi, ti: (bi, 0, ti)),
                 pl.BlockSpec((1, GQA_KV_HEADS, tile, hd), lambda bi, ti: (bi, 0, ti, 0)),
                 pl.BlockSpec((1, GQA_KV_HEADS * V_ROWS, tile), lambda bi, ti: (bi, 0, ti))]
    out_shape = [jax.ShapeDtypeStruct((b, nq, n), BF16),
                 jax.ShapeDtypeStruct((b, GQA_KV_HEADS, n, hd), BF16),
                 jax.ShapeDtypeStruct((b, GQA_KV_HEADS * V_ROWS, n), BF16)]
    if natural_in:
        out_specs.append(pl.BlockSpec((1, d, tile), lambda bi, ti: (bi, 0, ti)))
        out_shape.append(jax.ShapeDtypeStruct((b, d, n), F32))
    return pl.pallas_call(
        functools.partial(_gqa_proj_kernel, rotate=rotate, natural_in=natural_in, mod_row=mod_row),
        grid=(b, n // tile),
        in_specs=in_specs,
        out_specs=out_specs,
        out_shape=out_shape,
        scratch_shapes=[pltpu.VMEM((2, d, LANES), F32)],
        compiler_params=_params(2),
        name="gqa_proj_lat" if rotate else "gqa_proj_ctx",
    )(*args)


def _mla_proj_kernel(x_ref, mod_ref, win_ref, gq_ref, gkv_ref, wuq_ref, wukv_ref, *rest, rotate, mod_row):
    if rotate:
        rope_ref, q_ref, k_ref, v_ref, cols_ref = rest
    else:
        q_ref, k_ref, v_ref, cols_ref = rest
    _cache_mod_columns(cols_ref, [mod_ref], mod_row)
    nh, nope, rd, pad = MLA_HEADS, MLA_NOPE_DIM, MLA_ROPE_DIM, MLA_QK_PAD
    t = x_ref.shape[2]
    h = _modulate(x_ref[0], cols_ref).astype(BF16)
    p = jnp.dot(win_ref[0], h, preferred_element_type=F32)
    cq = _rms_rows(p[0:MLA_Q_RANK], gq_ref[...]).astype(BF16)
    ckv = _rms_rows(p[MLA_Q_RANK:MLA_Q_RANK + MLA_KV_RANK], gkv_ref[...]).astype(BF16)
    k_pe = p[MLA_Q_RANK + MLA_KV_RANK:MLA_Q_RANK + MLA_KV_RANK + rd]
    if rotate:
        k_pe = _rotate_rows(k_pe, rope_ref)
    q = jnp.dot(wuq_ref[0], cq, preferred_element_type=F32)
    kv = jnp.dot(wukv_ref[0], ckv, preferred_element_type=F32)
    q_scale = MLA_QK_DIM ** -0.5 * LOG2E
    zeros = jnp.zeros((pad - nope - rd, t), F32)
    qk = MLA_QK_DIM
    for i in range(nh):
        q_pe = q[i * qk + nope:(i + 1) * qk]
        if rotate:
            q_pe = _rotate_rows(q_pe, rope_ref)
        qh = jnp.concatenate([q[i * qk:i * qk + nope], q_pe], axis=0) * q_scale
        q_ref[0, i * pad:(i + 1) * pad, :] = jnp.concatenate([qh, zeros], axis=0).astype(BF16)
        kh = jnp.concatenate([kv[i * nope:(i + 1) * nope], k_pe, zeros], axis=0)
        k_ref[0, i] = kh.T.astype(BF16)
    _store_values(v_ref, kv[nh * nope:], nh)


def _mla_proj(x_t, mod, layer, mod_row, w_idx, win_t, gq_col, gkv_col, wuq_t, wukv_t, rope, tile):
    b, d, n = x_t.shape
    nh, pad = MLA_HEADS, MLA_QK_PAD
    rotate = rope is not None
    in_specs = [pl.BlockSpec((1, d, tile), lambda bi, ti: (bi, 0, ti)),
                pl.BlockSpec((1, MOD_ROWS, 2 * d), lambda bi, ti: (layer, 0, 0)),
                _layer_spec(win_t, w_idx),
                pl.BlockSpec(gq_col.shape, lambda bi, ti: (0, 0)),
                pl.BlockSpec(gkv_col.shape, lambda bi, ti: (0, 0)),
                _layer_spec(wuq_t, w_idx),
                _layer_spec(wukv_t, w_idx)]
    args = [x_t, mod, win_t, gq_col, gkv_col, wuq_t, wukv_t]
    if rotate:
        in_specs.append(pl.BlockSpec((4, MLA_ROPE_DIM // 4, tile), lambda bi, ti: (0, 0, ti)))
        args.append(rope)
    return pl.pallas_call(
        functools.partial(_mla_proj_kernel, rotate=rotate, mod_row=mod_row),
        grid=(b, n // tile),
        in_specs=in_specs,
        scratch_shapes=[pltpu.VMEM((2, d, LANES), F32)],
        out_specs=[pl.BlockSpec((1, nh * pad, tile), lambda bi, ti: (bi, 0, ti)),
                   pl.BlockSpec((1, nh, tile, pad), lambda bi, ti: (bi, 0, ti, 0)),
                   pl.BlockSpec((1, nh * V_ROWS, tile), lambda bi, ti: (bi, 0, ti))],
        out_shape=[jax.ShapeDtypeStruct((b, nh * pad, n), BF16),
                   jax.ShapeDtypeStruct((b, nh, n, pad), BF16),
                   jax.ShapeDtypeStruct((b, nh * V_ROWS, n), BF16)],
        compiler_params=_params(2),
        name="mla_proj_lat" if rotate else "mla_proj_ctx",
    )(*args)


def _attn_kernel(q_ref, *rest, n_heads, group, dk_rows, dv):
    n_src = (len(rest) - 5) // 3
    k_refs = rest[0:2 * n_src:2]
    v_refs = rest[1:2 * n_src:2]
    qn_ref = rest[2 * n_src]
    kn_refs = rest[2 * n_src + 1:3 * n_src + 1]
    o_ref = rest[3 * n_src + 1]
    s_refs = rest[3 * n_src + 2:3 * n_src + 4]
    m_ref = rest[3 * n_src + 4]
    chunks = []
    base = 0
    for src, k_ref in enumerate(k_refs):
        for lo in range(0, k_ref.shape[2], KV_CHUNK):
            chunks.append((src, lo, base + lo))
        base += k_ref.shape[2]

    tq = s_refs[0].shape[1]
    q_tiles = q_ref.shape[2] // tq
    n_units = n_heads * q_tiles

    def lanes(u):
        qt = u // n_heads
        return slice(qt * tq, (qt + 1) * tq) if isinstance(u, int) else pl.ds(pl.multiple_of(qt * tq, tq), tq)

    def unit_of_this_step(u, slot):
        h = u % n_heads
        kvh = h // group
        q = q_ref[0, pl.ds(pl.multiple_of(h * dk_rows, dk_rows), dk_rows), lanes(u)]
        return q, (lambda src, lo: k_refs[src][0, kvh, lo:lo + KV_CHUNK, :]), slot

    def unit_0_of_next_step():
        return qn_ref[0], (lambda src, lo: kn_refs[src][0, 0, lo:lo + KV_CHUNK, :]), 0

    def step(cur, nxt):
        q, load_keys, slot_n = nxt
        if cur is not None:
            u_c, slot_c, m_c = cur
            h_c = u_c % n_heads
            v_row = pl.multiple_of((h_c // group) * V_ROWS, ONES_ROWS)
        m_n = None
        acc = None
        for src, lo, glo in chunks:
            s = jnp.dot(load_keys(src, lo), q, preferred_element_type=F32)
            s_refs[slot_n][glo:glo + KV_CHUNK, :] = s
            ms = jnp.max(s, axis=0, keepdims=True)
            m_n = ms if m_n is None else jnp.maximum(m_n, ms)
            if cur is not None:
                p = jnp.exp2(s_refs[slot_c][glo:glo + KV_CHUNK, :] - m_c).astype(BF16)
                part = jnp.dot(v_refs[src][0, pl.ds(v_row, V_ROWS), lo:lo + KV_CHUNK], p,
                               preferred_element_type=F32)
                acc = part if acc is None else acc + part
        if cur is not None:
            out = acc[0:dv] / acc[dv:dv + 1]
            o_ref[0, pl.ds(pl.multiple_of(h_c * dv, dv), dv), lanes(u_c)] = out.astype(BF16)
        return m_n

    @pl.when(jnp.logical_and(pl.program_id(0) == 0, pl.program_id(1) == 0))
    def _():
        m_ref[...] = step(None, unit_of_this_step(0, 0))

    def run(i, m):
        u0 = i * HEADS_PER_LOOP
        for t in range(HEADS_PER_LOOP):
            m = step((u0 + t, t % 2, m), unit_of_this_step(u0 + t + 1, (t + 1) % 2))
        return m

    n_loop = n_units // HEADS_PER_LOOP - 1
    m = lax.fori_loop(0, jnp.minimum(pl.program_id(0) + n_loop, n_loop), run, m_ref[...])
    u0 = n_loop * HEADS_PER_LOOP
    for t in range(HEADS_PER_LOOP - 1):
        m = step((u0 + t, t % 2, m), unit_of_this_step(u0 + t + 1, (t + 1) % 2))
    m_ref[...] = step((n_units - 1, 1, m), unit_0_of_next_step())


def _attention(q_t, kv_sources, batch, n_heads, group, dk_rows, dv, tq, name):
    qb, qrows, nq = q_t.shape
    tq_step = tq * Q_TILES_PER_STEP
    tiles = nq // tq_step
    grid = (batch, tiles)

    def next_step(bi, ti):
        flat = jnp.minimum(bi * tiles + ti + 1, batch * tiles - 1)
        return flat // tiles, flat % tiles

    q_index = lambda bi, ti: (bi, 0, ti)
    qn_index = lambda bi, ti: (next_step(bi, ti)[0], 0, next_step(bi, ti)[1] * Q_TILES_PER_STEP)
    in_specs = [pl.BlockSpec((1, qrows, tq_step), q_index)]
    args = [q_t]
    next_specs = [pl.BlockSpec((1, dk_rows, tq), qn_index)]
    next_args = [q_t]
    n_keys = 0
    for k, v, keys in kv_sources:
        if k.shape[0] == 1:
            k_index, v_index = (lambda bi, ti: (0, 0, bi, 0)), (lambda bi, ti: (0, 0, bi))
            kn_index = lambda bi, ti: (0, 0, next_step(bi, ti)[0], 0)
        else:
            k_index, v_index = (lambda bi, ti: (bi, 0, 0, 0)), (lambda bi, ti: (bi, 0, 0))
            kn_index = lambda bi, ti: (next_step(bi, ti)[0], 0, 0, 0)
        in_specs.append(pl.BlockSpec((1, k.shape[1], keys, k.shape[3]), k_index))
        in_specs.append(pl.BlockSpec((1, v.shape[1], keys), v_index))
        args += [k, v]
        next_specs.append(pl.BlockSpec((1, 1, keys, k.shape[3]), kn_index))
        next_args.append(k)
        n_keys += keys
    return pl.pallas_call(
        functools.partial(_attn_kernel, n_heads=n_heads, group=group, dk_rows=dk_rows, dv=dv),
        grid=grid,
        in_specs=in_specs + next_specs,
        out_specs=pl.BlockSpec((1, n_heads * dv, tq_step), q_index),
        out_shape=jax.ShapeDtypeStruct((qb, n_heads * dv, nq), BF16),
        scratch_shapes=[pltpu.VMEM((n_keys, tq), F32), pltpu.VMEM((n_keys, tq), F32),
                        pltpu.VMEM((1, tq), F32)],
        compiler_params=_params(2),
        name=name,
    )(*args, *next_args)


def _ctx_attn_kernel(q_ref, k_ref, v_ref, o_ref, s_ref, *, n_heads, group, dk_rows, dv):
    maxes = []
    for h in range(n_heads):
        q = q_ref[0, h * dk_rows:(h + 1) * dk_rows, :]
        s = jnp.dot(k_ref[0, h // group], q, preferred_element_type=F32)
        s_ref[h] = s
        maxes.append(jnp.max(s, axis=0, keepdims=True))
    for h in range(n_heads):
        p = jnp.exp2(s_ref[h] - maxes[h]).astype(BF16)
        kvh = h // group
        acc = jnp.dot(v_ref[0, kvh * V_ROWS:(kvh + 1) * V_ROWS, :], p, preferred_element_type=F32)
        o_ref[0, h * dv:(h + 1) * dv, :] = (acc[0:dv] / acc[dv:dv + 1]).astype(BF16)


def _ctx_attention(q_t, k, v, batch, n_heads, group, dk_rows, dv):
    _, qrows, total = q_t.shape
    n = total // batch
    return pl.pallas_call(
        functools.partial(_ctx_attn_kernel, n_heads=n_heads, group=group, dk_rows=dk_rows, dv=dv),
        grid=(batch,),
        in_specs=[pl.BlockSpec((1, qrows, n), lambda bi: (0, 0, bi)),
                  pl.BlockSpec((1, k.shape[1], n, k.shape[3]), lambda bi: (0, 0, bi, 0)),
                  pl.BlockSpec((1, v.shape[1], n), lambda bi: (0, 0, bi))],
        out_specs=pl.BlockSpec((1, n_heads * dv, n), lambda bi: (0, 0, bi)),
        out_shape=jax.ShapeDtypeStruct((1, n_heads * dv, total), BF16),
        scratch_shapes=[pltpu.VMEM((n_heads, n, n), F32)],
        compiler_params=_params(1),
        name="attn_ctx",
    )(q_t, k, v)


def _post_kernel(x_ref, a_ref, moda_ref, modm_ref, ln_ref, wo_ref, w1_ref, w2_ref, o_ref, cols_ref, *,
                 natural_out, mod_row):
    d = D_MODEL
    t = x_ref.shape[2]
    half = t // 2
    pieces = half // LANES
    n_chunks = FFN_HIDDEN // FFN_CHUNK
    _cache_mod_columns(cols_ref, [moda_ref, modm_ref], mod_row)

    def col(i, width):
        return _wide(cols_ref[i] if i < 4 else ln_ref[0, i - 4], width)

    def merge_attn(lo, width, y):
        x1 = _layer_norm_rows(DEEPNORM_ALPHA * x_ref[0, :, lo:lo + width] + col(0, width) * y,
                              col(4, width), col(5, width))
        return x1, (x1 * (1.0 + col(2, width)) + col(1, width)).astype(BF16)

    def mlp_chunk(c, h):
        u = jnp.dot(w1_ref[0, c * FFN_CHUNK:(c + 1) * FFN_CHUNK, :], h, preferred_element_type=F32)
        u = jnp.maximum(u, 0.0)
        u = (u * u).astype(BF16)
        return jnp.dot(w2_ref[0, :, c * FFN_CHUNK:(c + 1) * FFN_CHUNK], u, preferred_element_type=F32)

    def merge_mlp(lo, x1, acc):
        x2 = _layer_norm_rows(DEEPNORM_ALPHA * x1 + col(3, LANES) * acc, col(6, LANES), col(7, LANES))
        if natural_out:
            o_ref[0, lo:lo + LANES, :] = x2.T
        else:
            o_ref[0, :, lo:lo + LANES] = x2

    y_a = jnp.dot(wo_ref[0], a_ref[0, :, 0:half], preferred_element_type=F32)
    y_b = jnp.dot(wo_ref[0], a_ref[0, :, half:t], preferred_element_type=F32)
    x1_a, h_a = merge_attn(0, half, y_a)
    acc_a = None
    x1_b, h_b = [], []
    for c in range(n_chunks):
        part = mlp_chunk(c, h_a)
        acc_a = part if acc_a is None else acc_a + part
        if c < pieces:
            x1, h = merge_attn(half + c * LANES, LANES, y_b[:, c * LANES:(c + 1) * LANES])
            x1_b.append(x1)
            h_b.append(h)
    h_b = jnp.concatenate(h_b, axis=1)
    acc_b = None
    for c in range(n_chunks):
        part = mlp_chunk(c, h_b)
        acc_b = part if acc_b is None else acc_b + part
        if c < pieces:
            merge_mlp(c * LANES, x1_a[:, c * LANES:(c + 1) * LANES], acc_a[:, c * LANES:(c + 1) * LANES])
    for j in range(pieces):
        merge_mlp(half + j * LANES, x1_b[j], acc_b[:, j * LANES:(j + 1) * LANES])


def _post(x_t, attn_t, mod, ln_cols, layer, mod_row, wo_t, wo_idx, w1_t, w2_t, tile, natural_out, name):
    b, d, n = x_t.shape
    single = dict(pipeline_mode=pl.Buffered(1))
    if natural_out:
        out_spec = pl.BlockSpec((1, tile, d), lambda bi, ti: (bi, ti, 0))
        out_shape = jax.ShapeDtypeStruct((b, n, d), F32)
    else:
        out_spec = pl.BlockSpec((1, d, tile), lambda bi, ti: (bi, 0, ti))
        out_shape = jax.ShapeDtypeStruct((b, d, n), F32)
    return pl.pallas_call(
        functools.partial(_post_kernel, natural_out=natural_out, mod_row=mod_row),
        grid=(b, n // tile),
        in_specs=[pl.BlockSpec((1, d, tile), lambda bi, ti: (bi, 0, ti)),
                  pl.BlockSpec((1, d, tile), lambda bi, ti: (bi, 0, ti)),
                  pl.BlockSpec((1, MOD_ROWS, 2 * d), lambda bi, ti: (layer, 0, 1)),
                  pl.BlockSpec((1, MOD_ROWS, 2 * d), lambda bi, ti: (layer, 0, 2)),
                  pl.BlockSpec((1, 4, d, LANES), lambda bi, ti: (layer, 0, 0, 0), **single),
                  _layer_spec(wo_t, wo_idx, **single),
                  _layer_spec(w1_t, layer, **single),
                  _layer_spec(w2_t, layer, **single)],
        out_specs=out_spec,
        out_shape=out_shape,
        scratch_shapes=[pltpu.VMEM((4, d, LANES), F32)],
        compiler_params=_params(2),
        name=name,
    )(x_t, attn_t, mod, mod, ln_cols, wo_t, w1_t, w2_t)


def _rope_tables_t(n_tok, rot_dim):
    rows = n_tok // GRID_W
    row = jnp.broadcast_to(jnp.arange(rows, dtype=F32)[:, None], (rows, GRID_W)).reshape(-1)
    col = jnp.broadcast_to(jnp.arange(GRID_W, dtype=F32)[None, :], (rows, GRID_W)).reshape(-1)
    axis_dim = rot_dim // 2
    inv_freq = ROPE_THETA ** (-jnp.arange(0, axis_dim, 2, dtype=F32) / axis_dim)
    ang_row = inv_freq[:, None] * row[None, :]
    ang_col = inv_freq[:, None] * col[None, :]
    return jnp.stack([jnp.cos(ang_row), jnp.sin(ang_row), jnp.cos(ang_col), jnp.sin(ang_col)])


def _col(v):
    return v.astype(F32)[:, None]


def kernel(x, c, ctx, c_ctx, w_ada, b_ada, ln_g, ln_b, mlp_w1, mlp_w2, gqa_w_qkv, gqa_q_norm, gqa_k_norm, gqa_w_o, mla_w_in, mla_q_norm, mla_kv_norm, mla_w_uq, mla_w_ukv, mla_w_o):
    b, n_lat, d = x.shape
    n_ctx = ctx.shape[1]

    cv = jnp.concatenate([c, c_ctx[None, :], jnp.zeros((MOD_ROWS - b - 1, d), F32)], axis=0)
    mod = _modulation(cv, w_ada, b_ada)
    ln_cols = jnp.stack([ln_g[:, 0], ln_b[:, 0], ln_g[:, 1], ln_b[:, 1]], axis=1)
    ln_cols = jnp.broadcast_to(ln_cols[..., None], ln_cols.shape + (LANES,))

    lat_mod = None
    ctx_mod = b

    rope_gqa = _rope_tables_t(n_lat, GQA_HEAD_DIM)
    rope_mla = _rope_tables_t(n_lat, MLA_ROPE_DIM)

    w1_t, w2_t = _transpose_cast(mlp_w1), _transpose_cast(mlp_w2)
    gqa_w_t, gqa_wo_t = _transpose_cast(gqa_w_qkv), _transpose_cast(gqa_w_o)
    mla_wo_t = _transpose_cast(mla_w_o)
    n_b = mla_w_in.shape[0]
    mla_win_t = mla_w_in.swapaxes(1, 2).astype(BF16)
    mla_wuq_t = mla_w_uq.swapaxes(1, 2).astype(BF16)
    wukv = mla_w_ukv.reshape(n_b, MLA_KV_RANK, MLA_HEADS, MLA_NOPE_DIM + MLA_V_DIM)
    wukv = jnp.concatenate([wukv[..., :MLA_NOPE_DIM].reshape(n_b, MLA_KV_RANK, -1),
                            wukv[..., MLA_NOPE_DIM:].reshape(n_b, MLA_KV_RANK, -1)], axis=2)
    mla_wukv_t = wukv.swapaxes(1, 2).astype(BF16)

    xc_nat = ctx.reshape(1, b * n_ctx, d)
    x_t = xc_t = None

    for i in range(DEPTH):
        need_ctx = i < DEPTH - 1
        last = i == DEPTH - 1
        j = i // N_MIXERS
        if i % N_MIXERS == 0:
            wo_t = gqa_wo_t
            gq, gk = _col(gqa_q_norm[j]), _col(gqa_k_norm[j])
            if i == 0:
                q_l, k_l, v_l, x_t = _gqa_proj(x, True, mod, i, lat_mod, gqa_w_t, j, gq, gk, rope_gqa, LAT_TILE)
                q_c, k_c, v_c, xc_t = _gqa_proj(xc_nat, True, mod, i, ctx_mod, gqa_w_t, j, gq, gk, None, LAT_TILE)
            else:
                q_l, k_l, v_l = _gqa_proj(x_t, False, mod, i, lat_mod, gqa_w_t, j, gq, gk, rope_gqa, LAT_TILE)
                q_c, k_c, v_c = _gqa_proj(xc_t, False, mod, i, ctx_mod, gqa_w_t, j, gq, gk, None, LAT_TILE)
            heads, group, dk_rows, dv = GQA_HEADS, GQA_HEADS // GQA_KV_HEADS, GQA_HEAD_DIM, GQA_HEAD_DIM
        else:
            wo_t = mla_wo_t
            gq, gkv = _col(mla_q_norm[j]), _col(mla_kv_norm[j])
            q_l, k_l, v_l = _mla_proj(x_t, mod, i, lat_mod, j, mla_win_t, gq, gkv, mla_wuq_t, mla_wukv_t,
                                      rope_mla, LAT_TILE)
            q_c, k_c, v_c = _mla_proj(xc_t, mod, i, ctx_mod, j, mla_win_t, gq, gkv, mla_wuq_t, mla_wukv_t,
                                      None, LAT_TILE)
            heads, group, dk_rows, dv = MLA_HEADS, 1, MLA_QK_PAD, MLA_V_DIM

        a_l = _attention(q_l, [(k_c, v_c, n_ctx), (k_l, v_l, n_lat)], b, heads, group, dk_rows, dv, LAT_TILE,
                         "attn_lat")
        x_t = _post(x_t, a_l, mod, ln_cols, i, lat_mod, wo_t, j, w1_t, w2_t, POST_TILE, last, "post_lat")
        if need_ctx:
            a_c = _ctx_attention(q_c, k_c, v_c, b, heads, group, dk_rows, dv)
            xc_t = _post(xc_t, a_c, mod, ln_cols, i, ctx_mod, wo_t, j, w1_t, w2_t, POST_TILE, False, "post_ctx")

    return x_t
```

```python
import functools

import jax
import jax.numpy as jnp
from jax import lax
from jax.experimental import pallas as pl
from jax.experimental.pallas import tpu as pltpu

D_MODEL = 1024
DEPTH = 4
GRID_W = 64
N_MIXERS = 2
GQA_HEADS = 16
GQA_KV_HEADS = 4
GQA_HEAD_DIM = 64
MLA_HEADS = 16
MLA_Q_RANK = 384
MLA_KV_RANK = 256
MLA_NOPE_DIM = 64
MLA_ROPE_DIM = 32
MLA_V_DIM = 64
MLA_QK_DIM = MLA_NOPE_DIM + MLA_ROPE_DIM
MLA_QK_PAD = 128
FFN_HIDDEN = 4 * D_MODEL
ROPE_THETA = 10000.0
NORM_EPS = 1e-6
DEEPNORM_ALPHA = (2.0 * DEPTH) ** 0.25
LOG2E = 1.4426950408889634

HEAD_V_DIM = 64
ONES_ROWS = 16
V_ROWS = HEAD_V_DIM + ONES_ROWS

LAT_TILE = 512
PROJ_TILE = 1024
POST_TILE = 1024
KV_CHUNK = 256
HEADS_PER_LOOP = 8
Q_TILES_PER_STEP = 2
FFN_CHUNK = 1024
WEIGHT_TILE = 1024
MOD_ROWS = 16
PROJ_ROWS = 256
LANES = 128
VMEM_LIMIT_BYTES = 60 * 1024 * 1024

F32 = jnp.float32
BF16 = jnp.bfloat16


def _params(n_axes):
    return pltpu.CompilerParams(dimension_semantics=("arbitrary",) * n_axes,
                                vmem_limit_bytes=VMEM_LIMIT_BYTES)


def _transpose_cast_kernel(w_ref, o_ref):
    o_ref[0] = w_ref[0].T.astype(BF16)


def _transpose_cast(w):
    n_l, k, m = w.shape
    tk = WEIGHT_TILE if k % WEIGHT_TILE == 0 else WEIGHT_TILE // 2
    tm = WEIGHT_TILE if m % WEIGHT_TILE == 0 else WEIGHT_TILE // 2
    return pl.pallas_call(
        _transpose_cast_kernel,
        grid=(n_l, k // tk, m // tm),
        in_specs=[pl.BlockSpec((1, tk, tm), lambda l, i, j: (l, i, j))],
        out_specs=pl.BlockSpec((1, tm, tk), lambda l, i, j: (l, j, i)),
        out_shape=jax.ShapeDtypeStruct((n_l, m, k), BF16),
        compiler_params=_params(3),
        name="weight_transpose_cast",
    )(w)


def _layer_spec(w, idx, **kw):
    return pl.BlockSpec((1,) + w.shape[1:], lambda bi, ti: (idx, 0, 0), **kw)


def _mod_kernel(cv_ref, w_ref, b_ref, o_ref):
    c = cv_ref[...]
    s = c * jax.nn.sigmoid(c)
    w = w_ref[0]
    s_hi = s.astype(BF16)
    s_lo = (s - s_hi.astype(F32)).astype(BF16)
    w_hi = w.astype(BF16)
    w_lo = (w - w_hi.astype(F32)).astype(BF16)
    dot = functools.partial(jnp.dot, preferred_element_type=F32)
    o_ref[0] = dot(s_hi, w_hi) + (dot(s_lo, w_hi) + dot(s_hi, w_lo)) + b_ref[0]


def _modulation(cv, w_ada, b_ada):
    d = D_MODEL
    return pl.pallas_call(
        _mod_kernel,
        grid=(DEPTH, 6),
        in_specs=[pl.BlockSpec((MOD_ROWS, d), lambda i, j: (0, 0)),
                  pl.BlockSpec((1, d, d), lambda i, j: (i, 0, j)),
                  pl.BlockSpec((1, 1, d), lambda i, j: (i, 0, j))],
        out_specs=pl.BlockSpec((1, MOD_ROWS, d), lambda i, j: (i, 0, j)),
        out_shape=jax.ShapeDtypeStruct((DEPTH, MOD_ROWS, 6 * d), F32),
        compiler_params=_params(2),
        name="adaln_modulation",
    )(cv, w_ada, b_ada.reshape(DEPTH, 1, 6 * d))


def _columns(row):
    return jnp.broadcast_to(row, (LANES, row.shape[1])).T


def _cache_mod_columns(cols_ref, mod_refs, mod_row):
    d = D_MODEL

    @pl.when(pl.program_id(1) == 0)
    def _():
        r = pl.program_id(0) if mod_row is None else mod_row
        for i, mod_ref in enumerate(mod_refs):
            cols = _columns(mod_ref[0, pl.ds(r, 1), :])
            cols_ref[2 * i] = cols[0:d]
            cols_ref[2 * i + 1] = cols[d:2 * d]


def _wide(cols, t):
    return cols if t == LANES else jnp.concatenate([cols] * (t // LANES), axis=1)


def _modulate(x, cols_ref):
    t = x.shape[1]
    return x * (1.0 + _wide(cols_ref[1], t)) + _wide(cols_ref[0], t)


def _rms_rows(x, gain_col):
    ms = jnp.mean(x * x, axis=0, keepdims=True)
    return x * lax.rsqrt(ms + NORM_EPS) * gain_col


def _layer_norm_rows(z, g_col, b_col):
    mu = jnp.mean(z, axis=0, keepdims=True)
    zc = z - mu
    var = jnp.mean(zc * zc, axis=0, keepdims=True)
    return zc * lax.rsqrt(var + NORM_EPS) * g_col + b_col


def _rotate_rows(x, rope_ref):
    q = x.shape[0] // 4
    cr, sr, cc, sc = rope_ref[0], rope_ref[1], rope_ref[2], rope_ref[3]
    a1, a2, b1, b2 = x[0:q], x[q:2 * q], x[2 * q:3 * q], x[3 * q:4 * q]
    return jnp.concatenate([a1 * cr - a2 * sr, a2 * cr + a1 * sr,
                            b1 * cc - b2 * sc, b2 * cc + b1 * sc], axis=0)


def _store_values(v_ref, v, n_heads):
    ones = jnp.ones((ONES_ROWS, v.shape[1]), BF16)
    for j in range(n_heads):
        v_ref[0, j * V_ROWS:j * V_ROWS + HEAD_V_DIM, :] = v[j * HEAD_V_DIM:(j + 1) * HEAD_V_DIM].astype(BF16)
        v_ref[0, j * V_ROWS + HEAD_V_DIM:(j + 1) * V_ROWS, :] = ones


def _gqa_proj_kernel(x_ref, mod_ref, w_ref, gq_ref, gk_ref, *rest, rotate, natural_in, mod_row):
    rest = list(rest)
    rope_ref = rest.pop(0) if rotate else None
    cols_ref = rest.pop()
    q_ref, k_ref, v_ref = rest[:3]
    d = D_MODEL
    hd = GQA_HEAD_DIM
    nq = GQA_HEADS * hd
    nk = GQA_KV_HEADS * hd
    if natural_in:
        x = x_ref[0]
        r = pl.program_id(0) if mod_row is None else mod_row
        row = mod_ref[0, pl.ds(r, 1), :]
        h = (x * (1.0 + row[:, d:2 * d]) + row[:, 0:d]).astype(BF16)
        rest[3][0] = x.T
        dims = (((1,), (1,)), ((), ()))
    else:
        _cache_mod_columns(cols_ref, [mod_ref], mod_row)
        h = _modulate(x_ref[0], cols_ref).astype(BF16)
        dims = (((1,), (0,)), ((), ()))
    gq = gq_ref[...]
    gk = gk_ref[...]
    q_scale = hd ** -0.5 * LOG2E
    heads_per_step = PROJ_ROWS // hd

    def project(row0, rows):
        return lax.dot_general(w_ref[0, row0:row0 + rows, :], h, dims, preferred_element_type=F32)

    for g in range(nq // PROJ_ROWS):
        p = project(g * PROJ_ROWS, PROJ_ROWS)
        for i in range(heads_per_step):
            qh = _rms_rows(p[i * hd:(i + 1) * hd], gq)
            if rotate:
                qh = _rotate_rows(qh, rope_ref)
            row = g * PROJ_ROWS + i * hd
            q_ref[0, row:row + hd, :] = (qh * q_scale).astype(BF16)
    p = project(nq, nk)
    ks = []
    for j in range(GQA_KV_HEADS):
        kh = _rms_rows(p[j * hd:(j + 1) * hd], gk)
        if rotate:
            kh = _rotate_rows(kh, rope_ref)
        ks.append(kh)
    kt = jnp.concatenate(ks, axis=0).T
    for j in range(GQA_KV_HEADS):
        k_ref[0, j] = kt[:, j * hd:(j + 1) * hd].astype(BF16)
    _store_values(v_ref, project(nq + nk, nk), GQA_KV_HEADS)


def _gqa_proj(x, natural_in, mod, layer, mod_row, w_t, w_idx, gq_col, gk_col, rope, tile):
    if natural_in:
        b, n, d = x.shape
        x_spec = pl.BlockSpec((1, tile, d), lambda bi, ti: (bi, ti, 0))
    else:
        b, d, n = x.shape
        x_spec = pl.BlockSpec((1, d, tile), lambda bi, ti: (bi, 0, ti))
    hd = GQA_HEAD_DIM
    nq = GQA_HEADS * hd
    rotate = rope is not None
    in_specs = [x_spec,
                pl.BlockSpec((1, MOD_ROWS, 2 * d), lambda bi, ti: (layer, 0, 0)),
                _layer_spec(w_t, w_idx),
                pl.BlockSpec(gq_col.shape, lambda bi, ti: (0, 0)),
                pl.BlockSpec(gk_col.shape, lambda bi, ti: (0, 0))]
    args = [x, mod, w_t, gq_col, gk_col]
    if rotate:
        in_specs.append(pl.BlockSpec((4, hd // 4, tile), lambda bi, ti: (0, 0, ti)))
        args.append(rope)
    out_specs = [pl.BlockSpec((1, nq, tile), lambda bi, ti: (bi, 0, ti)),
                 pl.BlockSpec((1, GQA_KV_HEADS, tile, hd), lambda bi, ti: (bi, 0, ti, 0)),
                 pl.BlockSpec((1, GQA_KV_HEADS * V_ROWS, tile), lambda bi, ti: (bi, 0, ti))]
    out_shape = [jax.ShapeDtypeStruct((b, nq, n), BF16),
                 jax.ShapeDtypeStruct((b, GQA_KV_HEADS, n, hd), BF16),
                 jax.ShapeDtypeStruct((b, GQA_KV_HEADS * V_ROWS, n), BF16)]
    if natural_in:
        out_specs.append(pl.BlockSpec((1, d, tile), lambda bi, ti: (bi, 0, ti)))
        out_shape.append(jax.ShapeDtypeStruct((b, d, n), F32))
    return pl.pallas_call(
        functools.partial(_gqa_proj_kernel, rotate=rotate, natural_in=natural_in, mod_row=mod_row),
        grid=(b, n // tile),
        in_specs=in_specs,
        out_specs=out_specs,
        out_shape=out_shape,
        scratch_shapes=[pltpu.VMEM((2, d, LANES), F32)],
        compiler_params=_params(2),
        name="gqa_proj_lat" if rotate else "gqa_proj_ctx",
    )(*args)


def _mla_proj_kernel(x_ref, mod_ref, win_ref, gq_ref, gkv_ref, wuq_ref, wukv_ref, *rest, rotate, mod_row):
    if rotate:
        rope_ref, q_ref, k_ref, v_ref, cols_ref = rest
    else:
        q_ref, k_ref, v_ref, cols_ref = rest
    _cache_mod_columns(cols_ref, [mod_ref], mod_row)
    nh, nope, rd, pad = MLA_HEADS, MLA_NOPE_DIM, MLA_ROPE_DIM, MLA_QK_PAD
    t = x_ref.shape[2]
    h = _modulate(x_ref[0], cols_ref).astype(BF16)
    p = jnp.dot(win_ref[0], h, preferred_element_type=F32)
    cq = _rms_rows(p[0:MLA_Q_RANK], gq_ref[...]).astype(BF16)
    ckv = _rms_rows(p[MLA_Q_RANK:MLA_Q_RANK + MLA_KV_RANK], gkv_ref[...]).astype(BF16)
    k_pe = p[MLA_Q_RANK + MLA_KV_RANK:MLA_Q_RANK + MLA_KV_RANK + rd]
    if rotate:
        k_pe = _rotate_rows(k_pe, rope_ref)
    q = jnp.dot(wuq_ref[0], cq, preferred_element_type=F32)
    kv = jnp.dot(wukv_ref[0], ckv, preferred_element_type=F32)
    q_scale = MLA_QK_DIM ** -0.5 * LOG2E
    zeros = jnp.zeros((pad - nope - rd, t), F32)
    qk = MLA_QK_DIM
    for i in range(nh):
        q_pe = q[i * qk + nope:(i + 1) * qk]
        if rotate:
            q_pe = _rotate_rows(q_pe, rope_ref)
        qh = jnp.concatenate([q[i * qk:i * qk + nope], q_pe], axis=0) * q_scale
        q_ref[0, i * pad:(i + 1) * pad, :] = jnp.concatenate([qh, zeros], axis=0).astype(BF16)
        kh = jnp.concatenate([kv[i * nope:(i + 1) * nope], k_pe, zeros], axis=0)
        k_ref[0, i] = kh.T.astype(BF16)
    _store_values(v_ref, kv[nh * nope:], nh)


def _mla_proj(x_t, mod, layer, mod_row, w_idx, win_t, gq_col, gkv_col, wuq_t, wukv_t, rope, tile):
    b, d, n = x_t.shape
    nh, pad = MLA_HEADS, MLA_QK_PAD
    rotate = rope is not None
    in_specs = [pl.BlockSpec((1, d, tile), lambda bi, ti: (bi, 0, ti)),
                pl.BlockSpec((1, MOD_ROWS, 2 * d), lambda bi, ti: (layer, 0, 0)),
                _layer_spec(win_t, w_idx),
                pl.BlockSpec(gq_col.shape, lambda bi, ti: (0, 0)),
                pl.BlockSpec(gkv_col.shape, lambda bi, ti: (0, 0)),
                _layer_spec(wuq_t, w_idx),
                _layer_spec(wukv_t, w_idx)]
    args = [x_t, mod, win_t, gq_col, gkv_col, wuq_t, wukv_t]
    if rotate:
        in_specs.append(pl.BlockSpec((4, MLA_ROPE_DIM // 4, tile), lambda bi, ti: (0, 0, ti)))
        args.append(rope)
    return pl.pallas_call(
        functools.partial(_mla_proj_kernel, rotate=rotate, mod_row=mod_row),
        grid=(b, n // tile),
        in_specs=in_specs,
        scratch_shapes=[pltpu.VMEM((2, d, LANES), F32)],
        out_specs=[pl.BlockSpec((1, nh * pad, tile), lambda bi, ti: (bi, 0, ti)),
                   pl.BlockSpec((1, nh, tile, pad), lambda bi, ti: (bi, 0, ti, 0)),
                   pl.BlockSpec((1, nh * V_ROWS, tile), lambda bi, ti: (bi, 0, ti))],
        out_shape=[jax.ShapeDtypeStruct((b, nh * pad, n), BF16),
                   jax.ShapeDtypeStruct((b, nh, n, pad), BF16),
                   jax.ShapeDtypeStruct((b, nh * V_ROWS, n), BF16)],
        compiler_params=_params(2),
        name="mla_proj_lat" if rotate else "mla_proj_ctx",
    )(*args)


def _attn_kernel(q_ref, *rest, n_heads, group, dk_rows, dv):
    n_src = (len(rest) - 5) // 3
    k_refs = rest[0:2 * n_src:2]
    v_refs = rest[1:2 * n_src:2]
    qn_ref = rest[2 * n_src]
    kn_refs = rest[2 * n_src + 1:3 * n_src + 1]
    o_ref = rest[3 * n_src + 1]
    s_refs = rest[3 * n_src + 2:3 * n_src + 4]
    m_ref = rest[3 * n_src + 4]
    chunks = []
    base = 0
    for src, k_ref in enumerate(k_refs):
        for lo in range(0, k_ref.shape[2], KV_CHUNK):
            chunks.append((src, lo, base + lo))
        base += k_ref.shape[2]

    tq = s_refs[0].shape[1]
    q_tiles = q_ref.shape[2] // tq
    n_units = n_heads * q_tiles

    def lanes(u):
        qt = u // n_heads
        return slice(qt * tq, (qt + 1) * tq) if isinstance(u, int) else pl.ds(pl.multiple_of(qt * tq, tq), tq)

    def unit_of_this_step(u, slot):
        h = u % n_heads
        kvh = h // group
        q = q_ref[0, pl.ds(pl.multiple_of(h * dk_rows, dk_rows), dk_rows), lanes(u)]
        return q, (lambda src, lo: k_refs[src][0, kvh, lo:lo + KV_CHUNK, :]), slot

    def unit_0_of_next_step():
        return qn_ref[0], (lambda src, lo: kn_refs[src][0, 0, lo:lo + KV_CHUNK, :]), 0

    def step(cur, nxt):
        q, load_keys, slot_n = nxt
        if cur is not None:
            u_c, slot_c, m_c = cur
            h_c = u_c % n_heads
            v_row = pl.multiple_of((h_c // group) * V_ROWS, ONES_ROWS)
        m_n = None
        acc = None
        for src, lo, glo in chunks:
            s = jnp.dot(load_keys(src, lo), q, preferred_element_type=F32)
            s_refs[slot_n][glo:glo + KV_CHUNK, :] = s
            ms = jnp.max(s, axis=0, keepdims=True)
            m_n = ms if m_n is None else jnp.maximum(m_n, ms)
            if cur is not None:
                p = jnp.exp2(s_refs[slot_c][glo:glo + KV_CHUNK, :] - m_c).astype(BF16)
                part = jnp.dot(v_refs[src][0, pl.ds(v_row, V_ROWS), lo:lo + KV_CHUNK], p,
                               preferred_element_type=F32)
                acc = part if acc is None else acc + part
        if cur is not None:
            out = acc[0:dv] / acc[dv:dv + 1]
            o_ref[0, pl.ds(pl.multiple_of(h_c * dv, dv), dv), lanes(u_c)] = out.astype(BF16)
        return m_n

    @pl.when(jnp.logical_and(pl.program_id(0) == 0, pl.program_id(1) == 0))
    def _():
        m_ref[...] = step(None, unit_of_this_step(0, 0))

    def run(i, m):
        u0 = i * HEADS_PER_LOOP
        for t in range(HEADS_PER_LOOP):
            m = step((u0 + t, t % 2, m), unit_of_this_step(u0 + t + 1, (t + 1) % 2))
        return m

    n_loop = n_units // HEADS_PER_LOOP - 1
    m = lax.fori_loop(0, jnp.minimum(pl.program_id(0) + n_loop, n_loop), run, m_ref[...])
    u0 = n_loop * HEADS_PER_LOOP
    for t in range(HEADS_PER_LOOP - 1):
        m = step((u0 + t, t % 2, m), unit_of_this_step(u0 + t + 1, (t + 1) % 2))
    m_ref[...] = step((n_units - 1, 1, m), unit_0_of_next_step())


def _attention(q_t, kv_sources, batch, n_heads, group, dk_rows, dv, tq, name):
    qb, qrows, nq = q_t.shape
    tq_step = tq * Q_TILES_PER_STEP
    tiles = nq // tq_step
    grid = (batch, tiles)

    def next_step(bi, ti):
        flat = jnp.minimum(bi * tiles + ti + 1, batch * tiles - 1)
        return flat // tiles, flat % tiles

    q_index = lambda bi, ti: (bi, 0, ti)
    qn_index = lambda bi, ti: (next_step(bi, ti)[0], 0, next_step(bi, ti)[1] * Q_TILES_PER_STEP)
    in_specs = [pl.BlockSpec((1, qrows, tq_step), q_index)]
    args = [q_t]
    next_specs = [pl.BlockSpec((1, dk_rows, tq), qn_index)]
    next_args = [q_t]
    n_keys = 0
    for k, v, keys in kv_sources:
        if k.shape[0] == 1:
            k_index, v_index = (lambda bi, ti: (0, 0, bi, 0)), (lambda bi, ti: (0, 0, bi))
            kn_index = lambda bi, ti: (0, 0, next_step(bi, ti)[0], 0)
        else:
            k_index, v_index = (lambda bi, ti: (bi, 0, 0, 0)), (lambda bi, ti: (bi, 0, 0))
            kn_index = lambda bi, ti: (next_step(bi, ti)[0], 0, 0, 0)
        in_specs.append(pl.BlockSpec((1, k.shape[1], keys, k.shape[3]), k_index))
        in_specs.append(pl.BlockSpec((1, v.shape[1], keys), v_index))
        args += [k, v]
        next_specs.append(pl.BlockSpec((1, 1, keys, k.shape[3]), kn_index))
        next_args.append(k)
        n_keys += keys
    return pl.pallas_call(
        functools.partial(_attn_kernel, n_heads=n_heads, group=group, dk_rows=dk_rows, dv=dv),
        grid=grid,
        in_specs=in_specs + next_specs,
        out_specs=pl.BlockSpec((1, n_heads * dv, tq_step), q_index),
        out_shape=jax.ShapeDtypeStruct((qb, n_heads * dv, nq), BF16),
        scratch_shapes=[pltpu.VMEM((n_keys, tq), F32), pltpu.VMEM((n_keys, tq), F32),
                        pltpu.VMEM((1, tq), F32)],
        compiler_params=_params(2),
        name=name,
    )(*args, *next_args)


def _ctx_attn_kernel(q_ref, k_ref, v_ref, o_ref, s_ref, *, n_heads, group, dk_rows, dv):
    maxes = []
    for h in range(n_heads):
        q = q_ref[0, h * dk_rows:(h + 1) * dk_rows, :]
        s = jnp.dot(k_ref[0, h // group], q, preferred_element_type=F32)
        s_ref[h] = s
        maxes.append(jnp.max(s, axis=0, keepdims=True))
    for h in range(n_heads):
        p = jnp.exp2(s_ref[h] - maxes[h]).astype(BF16)
        kvh = h // group
        acc = jnp.dot(v_ref[0, kvh * V_ROWS:(kvh + 1) * V_ROWS, :], p, preferred_element_type=F32)
        o_ref[0, h * dv:(h + 1) * dv, :] = (acc[0:dv] / acc[dv:dv + 1]).astype(BF16)


def _ctx_attention(q_t, k, v, batch, n_heads, group, dk_rows, dv):
    _, qrows, total = q_t.shape
    n = total // batch
    return pl.pallas_call(
        functools.partial(_ctx_attn_kernel, n_heads=n_heads, group=group, dk_rows=dk_rows, dv=dv),
        grid=(batch,),
        in_specs=[pl.BlockSpec((1, qrows, n), lambda bi: (0, 0, bi)),
                  pl.BlockSpec((1, k.shape[1], n, k.shape[3]), lambda bi: (0, 0, bi, 0)),
                  pl.BlockSpec((1, v.shape[1], n), lambda bi: (0, 0, bi))],
        out_specs=pl.BlockSpec((1, n_heads * dv, n), lambda bi: (0, 0, bi)),
        out_shape=jax.ShapeDtypeStruct((1, n_heads * dv, total), BF16),
        scratch_shapes=[pltpu.VMEM((n_heads, n, n), F32)],
        compiler_params=_params(1),
        name="attn_ctx",
    )(q_t, k, v)


def _post_kernel(x_ref, a_ref, moda_ref, modm_ref, ln_ref, wo_ref, w1_ref, w2_ref, o_ref, cols_ref, *,
                 natural_out, mod_row):
    d = D_MODEL
    t = x_ref.shape[2]
    half = t // 2
    pieces = half // LANES
    n_chunks = FFN_HIDDEN // FFN_CHUNK
    _cache_mod_columns(cols_ref, [moda_ref, modm_ref], mod_row)

    def col(i, width):
        return _wide(cols_ref[i] if i < 4 else ln_ref[0, i - 4], width)

    def merge_attn(lo, width, y):
        x1 = _layer_norm_rows(DEEPNORM_ALPHA * x_ref[0, :, lo:lo + width] + col(0, width) * y,
                              col(4, width), col(5, width))
        return x1, (x1 * (1.0 + col(2, width)) + col(1, width)).astype(BF16)

    def mlp_chunk(c, h):
        u = jnp.dot(w1_ref[0, c * FFN_CHUNK:(c + 1) * FFN_CHUNK, :], h, preferred_element_type=F32)
        u = jnp.maximum(u, 0.0)
        u = (u * u).astype(BF16)
        return jnp.dot(w2_ref[0, :, c * FFN_CHUNK:(c + 1) * FFN_CHUNK], u, preferred_element_type=F32)

    def merge_mlp(lo, x1, acc):
        x2 = _layer_norm_rows(DEEPNORM_ALPHA * x1 + col(3, LANES) * acc, col(6, LANES), col(7, LANES))
        if natural_out:
            o_ref[0, lo:lo + LANES, :] = x2.T
        else:
            o_ref[0, :, lo:lo + LANES] = x2

    y_a = jnp.dot(wo_ref[0], a_ref[0, :, 0:half], preferred_element_type=F32)
    y_b = jnp.dot(wo_ref[0], a_ref[0, :, half:t], preferred_element_type=F32)
    x1_a, h_a = merge_attn(0, half, y_a)
    acc_a = None
    x1_b, h_b = [], []
    for c in range(n_chunks):
        part = mlp_chunk(c, h_a)
        acc_a = part if acc_a is None else acc_a + part
        if c < pieces:
            x1, h = merge_attn(half + c * LANES, LANES, y_b[:, c * LANES:(c + 1) * LANES])
            x1_b.append(x1)
            h_b.append(h)
    h_b = jnp.concatenate(h_b, axis=1)
    acc_b = None
    for c in range(n_chunks):
        part = mlp_chunk(c, h_b)
        acc_b = part if acc_b is None else acc_b + part
        if c < pieces:
            merge_mlp(c * LANES, x1_a[:, c * LANES:(c + 1) * LANES], acc_a[:, c * LANES:(c + 1) * LANES])
    for j in range(pieces):
        merge_mlp(half + j * LANES, x1_b[j], acc_b[:, j * LANES:(j + 1) * LANES])


def _post(x_t, attn_t, mod, ln_cols, layer, mod_row, wo_t, wo_idx, w1_t, w2_t, tile, natural_out, name):
    b, d, n = x_t.shape
    single = dict(pipeline_mode=pl.Buffered(1))
    if natural_out:
        out_spec = pl.BlockSpec((1, tile, d), lambda bi, ti: (bi, ti, 0))
        out_shape = jax.ShapeDtypeStruct((b, n, d), F32)
    else:
        out_spec = pl.BlockSpec((1, d, tile), lambda bi, ti: (bi, 0, ti))
        out_shape = jax.ShapeDtypeStruct((b, d, n), F32)
    return pl.pallas_call(
        functools.partial(_post_kernel, natural_out=natural_out, mod_row=mod_row),
        grid=(b, n // tile),
        in_specs=[pl.BlockSpec((1, d, tile), lambda bi, ti: (bi, 0, ti)),
                  pl.BlockSpec((1, d, tile), lambda bi, ti: (bi, 0, ti)),
                  pl.BlockSpec((1, MOD_ROWS, 2 * d), lambda bi, ti: (layer, 0, 1)),
                  pl.BlockSpec((1, MOD_ROWS, 2 * d), lambda bi, ti: (layer, 0, 2)),
                  pl.BlockSpec((1, 4, d, LANES), lambda bi, ti: (layer, 0, 0, 0), **single),
                  _layer_spec(wo_t, wo_idx, **single),
                  _layer_spec(w1_t, layer, **single),
                  _layer_spec(w2_t, layer, **single)],
        out_specs=out_spec,
        out_shape=out_shape,
        scratch_shapes=[pltpu.VMEM((4, d, LANES), F32)],
        compiler_params=_params(2),
        name=name,
    )(x_t, attn_t, mod, mod, ln_cols, wo_t, w1_t, w2_t)


def _rope_tables_t(n_tok, rot_dim):
    rows = n_tok // GRID_W
    row = jnp.broadcast_to(jnp.arange(rows, dtype=F32)[:, None], (rows, GRID_W)).reshape(-1)
    col = jnp.broadcast_to(jnp.arange(GRID_W, dtype=F32)[None, :], (rows, GRID_W)).reshape(-1)
    axis_dim = rot_dim // 2
    inv_freq = ROPE_THETA ** (-jnp.arange(0, axis_dim, 2, dtype=F32) / axis_dim)
    ang_row = inv_freq[:, None] * row[None, :]
    ang_col = inv_freq[:, None] * col[None, :]
    return jnp.stack([jnp.cos(ang_row), jnp.sin(ang_row), jnp.cos(ang_col), jnp.sin(ang_col)])


def _col(v):
    return v.astype(F32)[:, None]


def kernel(x, c, ctx, c_ctx, w_ada, b_ada, ln_g, ln_b, mlp_w1, mlp_w2, gqa_w_qkv, gqa_q_norm, gqa_k_norm, gqa_w_o, mla_w_in, mla_q_norm, mla_kv_norm, mla_w_uq, mla_w_ukv, mla_w_o):
    b, n_lat, d = x.shape
    n_ctx = ctx.shape[1]

    cv = jnp.concatenate([c, c_ctx[None, :], jnp.zeros((MOD_ROWS - b - 1, d), F32)], axis=0)
    mod = _modulation(cv, w_ada, b_ada)
    ln_cols = jnp.stack([ln_g[:, 0], ln_b[:, 0], ln_g[:, 1], ln_b[:, 1]], axis=1)
    ln_cols = jnp.broadcast_to(ln_cols[..., None], ln_cols.shape + (LANES,))

    lat_mod = None
    ctx_mod = b

    rope_gqa = _rope_tables_t(n_lat, GQA_HEAD_DIM)
    rope_mla = _rope_tables_t(n_lat, MLA_ROPE_DIM)

    w1_t, w2_t = _transpose_cast(mlp_w1), _transpose_cast(mlp_w2)
    gqa_w_t, gqa_wo_t = _transpose_cast(gqa_w_qkv), _transpose_cast(gqa_w_o)
    mla_wo_t = _transpose_cast(mla_w_o)
    n_b = mla_w_in.shape[0]
    mla_win_t = mla_w_in.swapaxes(1, 2).astype(BF16)
    mla_wuq_t = mla_w_uq.swapaxes(1, 2).astype(BF16)
    wukv = mla_w_ukv.reshape(n_b, MLA_KV_RANK, MLA_HEADS, MLA_NOPE_DIM + MLA_V_DIM)
    wukv = jnp.concatenate([wukv[..., :MLA_NOPE_DIM].reshape(n_b, MLA_KV_RANK, -1),
                            wukv[..., MLA_NOPE_DIM:].reshape(n_b, MLA_KV_RANK, -1)], axis=2)
    mla_wukv_t = wukv.swapaxes(1, 2).astype(BF16)

    xc_nat = ctx.reshape(1, b * n_ctx, d)
    x_t = xc_t = None

    for i in range(DEPTH):
        need_ctx = i < DEPTH - 1
        last = i == DEPTH - 1
        j = i // N_MIXERS
        if i % N_MIXERS == 0:
            wo_t = gqa_wo_t
            gq, gk = _col(gqa_q_norm[j]), _col(gqa_k_norm[j])
            if i == 0:
                q_l, k_l, v_l, x_t = _gqa_proj(x, True, mod, i, lat_mod, gqa_w_t, j, gq, gk, rope_gqa, PROJ_TILE)
                q_c, k_c, v_c, xc_t = _gqa_proj(xc_nat, True, mod, i, ctx_mod, gqa_w_t, j, gq, gk, None, PROJ_TILE)
            else:
                q_l, k_l, v_l = _gqa_proj(x_t, False, mod, i, lat_mod, gqa_w_t, j, gq, gk, rope_gqa, PROJ_TILE)
                q_c, k_c, v_c = _gqa_proj(xc_t, False, mod, i, ctx_mod, gqa_w_t, j, gq, gk, None, PROJ_TILE)
            heads, group, dk_rows, dv = GQA_HEADS, GQA_HEADS // GQA_KV_HEADS, GQA_HEAD_DIM, GQA_HEAD_DIM
        else:
            wo_t = mla_wo_t
            gq, gkv = _col(mla_q_norm[j]), _col(mla_kv_norm[j])
            q_l, k_l, v_l = _mla_proj(x_t, mod, i, lat_mod, j, mla_win_t, gq, gkv, mla_wuq_t, mla_wukv_t,
                                      rope_mla, PROJ_TILE)
            q_c, k_c, v_c = _mla_proj(xc_t, mod, i, ctx_mod, j, mla_win_t, gq, gkv, mla_wuq_t, mla_wukv_t,
                                      None, PROJ_TILE)
            heads, group, dk_rows, dv = MLA_HEADS, 1, MLA_QK_PAD, MLA_V_DIM

        a_l = _attention(q_l, [(k_c, v_c, n_ctx), (k_l, v_l, n_lat)], b, heads, group, dk_rows, dv, LAT_TILE,
                         "attn_lat")
        x_t = _post(x_t, a_l, mod, ln_cols, i, lat_mod, wo_t, j, w1_t, w2_t, POST_TILE, last, "post_lat")
        if need_ctx:
            a_c = _ctx_attention(q_c, k_c, v_c, b, heads, group, dk_rows, dv)
            xc_t = _post(xc_t, a_c, mod, ln_cols, i, ctx_mod, wo_t, j, w1_t, w2_t, POST_TILE, False, "post_ctx")

    return x_t
```
